```python
import jax, jax.numpy as jnp
from jax import lax
import numpy as np

D_MODEL = 4096
BATCH = 4
SEQ = 2048
DEPTH = 4
DEC_BATCH = 128
DEC_SEQ = 8
PAST_LEN = 16384
PAGE_SIZE = 128

HEAD_A = 64
D_A = D_MODEL // 2
H_A = D_A // HEAD_A
R_W = 96
R_A = 96
R_V = 64
R_G = 256
GN_EPS = 64e-5
CHUNK = 128
D_B = D_MODEL // 2
G_B = 16
C_G = D_B // G_B
P_A = 3 * D_A + R_W + R_A + R_G
P_IN = P_A + 2 * D_B + 2 * D_MODEL
_OFFS = (0, D_A, 2 * D_A, 3 * D_A, 3 * D_A + R_W, 3 * D_A + R_W + R_A, P_A)
D_FF = 11008
N_EXPERTS = 8
TOP_K = 2
N_DENSE = (DEPTH + 1) // 2
N_MOE = DEPTH // 2
ALPHA = (2 * DEPTH) ** 0.25
BETA = (8 * DEPTH) ** -0.25
LN_EPS = 1e-5

kernel_name = 'rwkv7_gmlp_gated_hybrid_step'

F32 = jnp.float32


def _ln(x, g, b, eps):
    xf = x.astype(F32)
    mu = jnp.mean(xf, axis=-1, keepdims=True)
    var = jnp.mean(jnp.square(xf - mu), axis=-1, keepdims=True)
    return (xf - mu) * lax.rsqrt(var + eps) * g.astype(F32) + b.astype(F32)


def _wkv7_scan(r, w, k, v, a, b, s0):
    def step(s, inp):
        r_t, w_t, k_t, v_t, a_t, b_t = inp
        sa = jnp.einsum('bhij,bhj->bhi', s, a_t)
        s = s * w_t[:, :, None, :] + sa[..., None] * b_t[:, :, None, :] + v_t[..., None] * k_t[:, :, None, :]
        return s, jnp.einsum('bhij,bhj->bhi', s, r_t)
    xs = tuple(jnp.swapaxes(t, 0, 1) for t in (r, w, k, v, a, b))
    s_last, ys = lax.scan(step, s0, xs)
    return jnp.swapaxes(ys, 0, 1), s_last


def _rwkv7_time_mix(pa, shift_prev, wkv_prev, v_first, layer, p):
    B, T, _ = pa.shape
    pa_prev = jnp.concatenate([shift_prev[:, None, :].astype(pa.dtype), pa[:, :-1]], axis=1)
    z = pa + p['mu_shift'][layer] * (pa_prev - pa)
    r, k, v, xw, xa, xg = (z[..., _OFFS[i]:_OFFS[i + 1]] for i in range(6))
    w_log = -jax.nn.softplus(-(p['w0'][layer] + jnp.tanh(xw) @ p['w2'][layer])) - 0.5
    decay = jnp.exp(-jnp.exp(w_log.astype(F32)))
    if layer == 0:
        v_first = v
    else:
        vl = layer - 1
        v = v + (v_first - v) * jax.nn.sigmoid(p['v0'][vl] + (v @ p['v1'][vl]) @ p['v2'][vl])
    a = jax.nn.sigmoid(p['a0'][layer] + xa @ p['a2'][layer])
    g = jax.nn.sigmoid(xg) @ p['g2'][layer]
    heads = lambda t: t.reshape(B, T, H_A, HEAD_A).astype(F32)
    kk = heads(k * p['k_k'][layer])
    kk = kk / jnp.maximum(jnp.sqrt(jnp.sum(kk * kk, axis=-1, keepdims=True)), 1e-12)
    k = k * (1 + (a - 1) * p['k_a'][layer])
    rh, kh, vh, ah = heads(r), heads(k), heads(v), heads(a)
    y, s_new = _wkv7_scan(rh, heads(decay), kh, vh, -kk, kk * ah, wkv_prev.astype(F32))
    mu = jnp.mean(y, axis=-1, keepdims=True)
    var = jnp.mean(jnp.square(y - mu), axis=-1, keepdims=True)
    y = ((y - mu) * lax.rsqrt(var + GN_EPS)).reshape(B, T, D_A) * p['lnx_g'][layer] + p['lnx_b'][layer]
    y = y + (jnp.sum(rh * kh * p['r_k'][layer].astype(F32), axis=-1, keepdims=True) * vh).reshape(B, T, D_A)
    out = (y * g) @ p['w_oa'][layer]
    return out, pa[:, -1], s_new, v_first


def _chunk_sgu(pb, layer, p):
    z = jax.nn.gelu(pb, approximate=False)
    u, v = z[..., :D_B], z[..., D_B:]
    v = _ln(v, p['sgu_ln_g'][layer], p['sgu_ln_b'][layer], LN_EPS)
    B, T, _ = v.shape
    L = min(T, CHUNK)
    n_chunks = -(-T // L)
    pad = n_chunks * L - T
    vc = jnp.pad(v, ((0, 0), (0, pad), (0, 0))).reshape(B, n_chunks, L, G_B, C_G)
    causal = jnp.tril(jnp.ones((L, L), dtype=bool))
    ws = jnp.where(causal[None], p['w_s'][layer][:, :L, :L], 0).astype(F32)
    bias = jnp.swapaxes(p['b_s'][layer][:, :L], 0, 1).astype(F32)
    sv = jnp.einsum('gts,bnsgc->bntgc', ws, vc) + bias[None, None, :, :, None]
    sv = sv.reshape(B, n_chunks * L, D_B)[:, :T]
    return (u * sv) @ p['w_ob'][layer], v


def _moe_swiglu(h, w_router, w1, w3, w2):
    B, T, D = h.shape
    ht = h.reshape(B * T, D)
    logits = (ht @ w_router).astype(F32)
    top_logit, top_idx = lax.top_k(logits, TOP_K)
    top_w = jax.nn.softmax(top_logit, axis=-1)
    combine = jnp.sum(jax.nn.one_hot(top_idx, N_EXPERTS, dtype=F32) * top_w[..., None], axis=1)
    out = jnp.zeros((B * T, D), F32)
    for e in range(N_EXPERTS):
        he = jax.nn.silu(ht @ w1[e]) * (ht @ w3[e])
        out = out + combine[:, e:e + 1] * (he @ w2[e])
    return out.reshape(B, T, D)


def _trunk(x, c, wkv_init, shift_init, p):
    wkv_out, shift_out, chunk_v_out = [], [], []
    v_first = None
    c_act = jax.nn.silu(c)
    for layer in range(DEPTH):
        mod = (c_act @ p['w_ada'][layer] + p['b_ada'][layer])[:, None, :]
        sh1, sc1, gt1, sh2, sc2, gt2 = jnp.split(mod, 6, axis=-1)
        h = x * (1 + sc1) + sh1
        proj = h @ p['w_in'][layer]
        pa = proj[..., :P_A]
        pb = proj[..., P_A:P_A + 2 * D_B]
        pg = proj[..., P_A + 2 * D_B:]
        ya, shift_new, wkv_new, v_first = _rwkv7_time_mix(pa, shift_init[layer], wkv_init[layer], v_first, layer, p)
        yb, v_rows = _chunk_sgu(pb, layer, p)
        gate = jax.nn.sigmoid(pg.astype(F32))
        merged = gate[..., :D_MODEL] * ya + gate[..., D_MODEL:] * yb
        mix = merged @ p['w_out'][layer]
        x = _ln(ALPHA * x + gt1 * mix, p['ln1_g'][layer], p['ln1_b'][layer], LN_EPS).astype(x.dtype)
        h2 = x * (1 + sc2) + sh2
        i = layer // 2
        if layer % 2 == 0:
            f = (jax.nn.silu(h2 @ p['ffn_w1'][i]) * (h2 @ p['ffn_w3'][i])) @ p['ffn_w2'][i]
        else:
            f = _moe_swiglu(h2, p['w_router'][i], p['moe_w1'][i], p['moe_w3'][i], p['moe_w2'][i])
        x = _ln(ALPHA * x + gt2 * f, p['ln2_g'][layer], p['ln2_b'][layer], LN_EPS).astype(x.dtype)
        wkv_out.append(wkv_new.astype(wkv_init.dtype))
        shift_out.append(shift_new.astype(shift_init.dtype))
        chunk_v_out.append(v_rows.astype(x.dtype))
    return x, jnp.stack(wkv_out), jnp.stack(shift_out), jnp.stack(chunk_v_out)


def setup_inputs(seed: int = 0) -> dict:
    key = jax.random.key(seed)
    ks = iter(jax.random.split(key, 64))
    nrm = lambda shape, s: jax.random.normal(next(ks), shape, F32) * s
    uni = lambda shape, lo, hi: jax.random.uniform(next(ks), shape, F32, lo, hi)
    d = D_MODEL
    return {
        'x_prompt': nrm((BATCH, SEQ, d), 1.0),
        'x_sample': nrm((DEC_BATCH, DEC_SEQ, d), 1.0),
        'c_prompt': nrm((BATCH, d), 1.0),
        'c_sample': nrm((DEC_BATCH, d), 1.0),
        'state_wkv': nrm((DEPTH, DEC_BATCH, H_A, HEAD_A, HEAD_A), 0.3),
        'state_shift': nrm((DEPTH, DEC_BATCH, P_A), 1.0),
        'w_ada': nrm((DEPTH, d, 6 * d), d ** -0.5),
        'b_ada': nrm((DEPTH, 6 * d), 0.01),
        'w_in': nrm((DEPTH, d, P_IN), d ** -0.5),
        'mu_shift': uni((DEPTH, P_A), 0.0, 1.0),
        'w0': uni((DEPTH, D_A), -6.0, -1.0),
        'w2': nrm((DEPTH, R_W, D_A), 0.1 * R_W ** -0.5),
        'a0': nrm((DEPTH, D_A), 0.1),
        'a2': nrm((DEPTH, R_A, D_A), 0.1 * R_A ** -0.5),
        'g2': nrm((DEPTH, R_G, D_A), R_G ** -0.5),
        'v0': nrm((DEPTH - 1, D_A), 0.1),
        'v1': nrm((DEPTH - 1, D_A, R_V), D_A ** -0.5),
        'v2': nrm((DEPTH - 1, R_V, D_A), 0.1 * R_V ** -0.5),
        'k_k': 1.0 + nrm((DEPTH, D_A), 0.1),
        'k_a': 1.0 + nrm((DEPTH, D_A), 0.1),
        'r_k': nrm((DEPTH, H_A, HEAD_A), 0.1),
        'lnx_g': 1.0 + nrm((DEPTH, D_A), 0.01),
        'lnx_b': nrm((DEPTH, D_A), 0.01),
        'w_oa': nrm((DEPTH, D_A, d), D_A ** -0.5),
        'sgu_ln_g': 1.0 + nrm((DEPTH, D_B), 0.01),
        'sgu_ln_b': nrm((DEPTH, D_B), 0.01),
        'w_s': nrm((DEPTH, G_B, CHUNK, CHUNK), CHUNK ** -0.5),
        'b_s': 1.0 + nrm((DEPTH, G_B, CHUNK), 0.01),
        'w_ob': nrm((DEPTH, D_B, d), D_B ** -0.5),
        'w_out': nrm((DEPTH, d, d), BETA * d ** -0.5),
        'ln1_g': 1.0 + nrm((DEPTH, d), 0.01),
        'ln1_b': nrm((DEPTH, d), 0.01),
        'ffn_w1': nrm((N_DENSE, d, D_FF), d ** -0.5),
        'ffn_w3': nrm((N_DENSE, d, D_FF), d ** -0.5),
        'ffn_w2': nrm((N_DENSE, D_FF, d), BETA * D_FF ** -0.5),
        'w_router': nrm((N_MOE, d, N_EXPERTS), d ** -0.5),
        'moe_w1': nrm((N_MOE, N_EXPERTS, d, D_FF), d ** -0.5),
        'moe_w3': nrm((N_MOE, N_EXPERTS, d, D_FF), d ** -0.5),
        'moe_w2': nrm((N_MOE, N_EXPERTS, D_FF, d), BETA * D_FF ** -0.5),
        'ln2_g': 1.0 + nrm((DEPTH, d), 0.01),
        'ln2_b': nrm((DEPTH, d), 0.01),
    }


def reference(x_prompt, x_sample, c_prompt, c_sample, state_wkv, state_shift, w_ada, b_ada, w_in, mu_shift,
              w0, w2, a0, a2, g2, v0, v1, v2, k_k, k_a, r_k, lnx_g, lnx_b, w_oa, sgu_ln_g, sgu_ln_b, w_s, b_s,
              w_ob, w_out, ln1_g, ln1_b, ffn_w1, ffn_w3, ffn_w2, w_router, moe_w1, moe_w3, moe_w2, ln2_g, ln2_b):
    p = dict(w_ada=w_ada, b_ada=b_ada, w_in=w_in, mu_shift=mu_shift, w0=w0, w2=w2, a0=a0, a2=a2, g2=g2,
             v0=v0, v1=v1, v2=v2, k_k=k_k, k_a=k_a, r_k=r_k, lnx_g=lnx_g, lnx_b=lnx_b, w_oa=w_oa,
             sgu_ln_g=sgu_ln_g, sgu_ln_b=sgu_ln_b, w_s=w_s, b_s=b_s, w_ob=w_ob, w_out=w_out,
             ln1_g=ln1_g, ln1_b=ln1_b, ffn_w1=ffn_w1, ffn_w3=ffn_w3, ffn_w2=ffn_w2, w_router=w_router,
             moe_w1=moe_w1, moe_w3=moe_w3, moe_w2=moe_w2, ln2_g=ln2_g, ln2_b=ln2_b)
    b_p = x_prompt.shape[0]
    wkv0 = jnp.zeros((DEPTH, b_p, H_A, HEAD_A, HEAD_A), state_wkv.dtype)
    shift0 = jnp.zeros((DEPTH, b_p, P_A), state_shift.dtype)
    y_prompt, wkv_p, shift_p, _ = _trunk(x_prompt, c_prompt, wkv0, shift0, p)
    y_sample, wkv_s, shift_s, chunk_v_s = _trunk(x_sample, c_sample, state_wkv, state_shift, p)
    return (y_prompt, y_sample, wkv_p, shift_p, wkv_s, shift_s, chunk_v_s)
```

```python
import functools

import jax
import jax.numpy as jnp
from jax import lax
from jax.experimental import pallas as pl
from jax.experimental.pallas import tpu as pltpu

F32 = jnp.float32
BF16 = jnp.bfloat16

LANES = 128
SUBLANES = 8
VMEM_LIMIT_BYTES = 56 * 1024 * 1024

HEAD = 64
CHUNK = 128
GROUP_C = 128
R_W, R_A, R_V, R_G = 96, 96, 64, 256
LORA_PAD = 512
TOP_K = 2
GN_EPS = 64e-5
LN_EPS = 1e-5


def _cparams(*sem):
    return pltpu.CompilerParams(dimension_semantics=sem, vmem_limit_bytes=VMEM_LIMIT_BYTES)


def _tile(n, target, mult):
    best = None
    for t in range(mult, min(n, target) + 1, mult):
        if n % t == 0:
            best = t
    return best if best is not None else n


def _mm_body(*refs, nk, has_bias):
    if has_bias:
        x_ref, w_ref, b_ref, o_ref, *scratch = refs
    else:
        x_ref, w_ref, o_ref, *scratch = refs

    def finish(acc):
        if has_bias:
            acc = acc + b_ref[...]
        o_ref[...] = acc.astype(o_ref.dtype)

    part = jnp.dot(x_ref[...].astype(BF16), w_ref[...].astype(BF16), preferred_element_type=F32)
    if nk == 1:
        finish(part)
    else:
        acc_ref, = scratch
        k = pl.program_id(2)

        @pl.when(k == 0)
        def _():
            acc_ref[...] = part

        @pl.when(k > 0)
        def _():
            acc_ref[...] += part

        @pl.when(k == nk - 1)
        def _():
            finish(acc_ref[...])


def _mm(x, w, *, layer=None, bias=None, out_dtype=F32, tm=1024, tn=512, nk=1, name="mm"):
    m, kdim = x.shape
    n = w.shape[-1]
    assert w.shape[-2] == kdim and kdim % nk == 0
    tm = _tile(m, tm, 16)
    tn = _tile(n, tn, LANES)
    tk = kdim // nk
    if layer is None:
        w_spec = pl.BlockSpec((tk, tn), lambda i, j, k: (k, j))
    else:
        w_spec = pl.BlockSpec((None, tk, tn), lambda i, j, k: (layer, k, j))
    in_specs = [pl.BlockSpec((tm, tk), lambda i, j, k: (i, k)), w_spec]
    args = [x, w]
    if bias is not None:
        in_specs.append(pl.BlockSpec((1, tn), lambda i, j, k: (0, j)))
        args.append(bias.reshape(1, n).astype(F32))
    return pl.pallas_call(
        functools.partial(_mm_body, nk=nk, has_bias=bias is not None),
        out_shape=jax.ShapeDtypeStruct((m, n), out_dtype),
        grid=(m // tm, n // tn, nk),
        in_specs=in_specs,
        out_specs=pl.BlockSpec((tm, tn), lambda i, j, k: (i, j)),
        scratch_shapes=[pltpu.VMEM((tm, tn), F32)] if nk > 1 else [],
        compiler_params=_cparams("parallel", "parallel", "arbitrary"),
        name=name,
    )(*args)


def _swiglu_body(x_ref, w1_ref, w3_ref, o_ref):
    x = x_ref[...].astype(BF16)
    a = jnp.dot(x, w1_ref[...].astype(BF16), preferred_element_type=F32)
    b = jnp.dot(x, w3_ref[...].astype(BF16), preferred_element_type=F32)
    o_ref[...] = (a * jax.nn.sigmoid(a) * b).astype(o_ref.dtype)


def _swiglu(x, w1, w3, layer, *, tm=1024, tn=256, name="ffn_up"):
    m, d = x.shape
    n = w1.shape[-1]
    tm = _tile(m, tm, 16)
    tn = _tile(n, tn, LANES)
    w_spec = pl.BlockSpec((None, d, tn), lambda i, j: (layer, 0, j))
    return pl.pallas_call(
        _swiglu_body,
        out_shape=jax.ShapeDtypeStruct((m, n), BF16),
        grid=(m // tm, n // tn),
        in_specs=[pl.BlockSpec((tm, d), lambda i, j: (i, 0)), w_spec, w_spec],
        out_specs=pl.BlockSpec((tm, tn), lambda i, j: (i, j)),
        compiler_params=_cparams("parallel", "parallel"),
        name=name,
    )(x, w1, w3)


def _moe_up_body(te_ref, nv_ref, x_ref, w1_ref, w3_ref, o_ref):
    @pl.when(pl.program_id(1) < nv_ref[0])
    def _():
        _swiglu_body(x_ref, w1_ref, w3_ref, o_ref)

    @pl.when(pl.program_id(1) >= nv_ref[0])
    def _():
        o_ref[...] = jnp.zeros_like(o_ref)


def _moe_up(xs, w1, w3, layer, tile_expert, n_valid, *, tm, tn=256, name="moe_up"):
    rs, d = xs.shape
    n = w1.shape[-1]
    tn = _tile(n, tn, LANES)
    w_spec = pl.BlockSpec((None, None, d, tn), lambda j, t, te, nv: (layer, te[t], 0, j))
    grid_spec = pltpu.PrefetchScalarGridSpec(
        num_scalar_prefetch=2,
        grid=(n // tn, rs // tm),
        in_specs=[pl.BlockSpec((tm, d), lambda j, t, te, nv: (t, 0)), w_spec, w_spec],
        out_specs=pl.BlockSpec((tm, tn), lambda j, t, te, nv: (t, j)),
    )
    return pl.pallas_call(
        _moe_up_body,
        out_shape=jax.ShapeDtypeStruct((rs, n), BF16),
        grid_spec=grid_spec,
        compiler_params=_cparams("parallel", "arbitrary"),
        name=name,
    )(tile_expert, n_valid, xs, w1, w3)


def _moe_down_body(te_ref, nv_ref, x_ref, w_ref, o_ref, acc_ref, *, nk):
    k = pl.program_id(2)
    valid = pl.program_id(1) < nv_ref[0]

    @pl.when(valid)
    def _():
        part = jnp.dot(x_ref[...], w_ref[...].astype(BF16), preferred_element_type=F32)

        @pl.when(k == 0)
        def _():
            acc_ref[...] = part

        @pl.when(k > 0)
        def _():
            acc_ref[...] += part

    @pl.when(jnp.logical_and(k == nk - 1, valid))
    def _():
        o_ref[...] = acc_ref[...]

    @pl.when(jnp.logical_and(k == nk - 1, jnp.logical_not(valid)))
    def _():
        o_ref[...] = jnp.zeros_like(o_ref)


def _moe_down(hs, w2, layer, tile_expert, n_valid, *, tm, tn=512, nk=2, name="moe_down"):
    rs, f = hs.shape
    n = w2.shape[-1]
    tn = _tile(n, tn, LANES)
    tk = f // nk
    assert f % nk == 0 and tk % LANES == 0
    grid_spec = pltpu.PrefetchScalarGridSpec(
        num_scalar_prefetch=2,
        grid=(n // tn, rs // tm, nk),
        in_specs=[pl.BlockSpec((tm, tk), lambda j, t, k, te, nv: (t, k)),
                  pl.BlockSpec((None, None, tk, tn), lambda j, t, k, te, nv: (layer, te[t], k, j))],
        out_specs=pl.BlockSpec((tm, tn), lambda j, t, k, te, nv: (t, j)),
        scratch_shapes=[pltpu.VMEM((tm, tn), F32)],
    )
    return pl.pallas_call(
        functools.partial(_moe_down_body, nk=nk),
        out_shape=jax.ShapeDtypeStruct((rs, n), F32),
        grid_spec=grid_spec,
        compiler_params=_cparams("parallel", "arbitrary", "arbitrary"),
        name=name,
    )(tile_expert, n_valid, hs, w2)


def _ln_body(x_ref, g_ref, b_ref, o_ref, *, eps):
    x = x_ref[...]
    mu = jnp.mean(x, axis=-1, keepdims=True)
    xc = x - mu
    var = jnp.mean(xc * xc, axis=-1, keepdims=True)
    o_ref[...] = (xc * lax.rsqrt(var + eps) * g_ref[...] + b_ref[...]).astype(o_ref.dtype)


def _layer_norm(x, g, b, eps, *, tm=256, name="ln"):
    m, c = x.shape
    tm = _tile(m, tm, SUBLANES)
    vec = pl.BlockSpec((1, c), lambda i: (0, 0))
    return pl.pallas_call(
        functools.partial(_ln_body, eps=eps),
        out_shape=jax.ShapeDtypeStruct((m, c), F32),
        grid=(m // tm,),
        in_specs=[pl.BlockSpec((tm, c), lambda i: (i, 0)), vec, vec],
        out_specs=pl.BlockSpec((tm, c), lambda i: (i, 0)),
        compiler_params=_cparams("parallel"),
        name=name,
    )(x, g.reshape(1, c).astype(F32), b.reshape(1, c).astype(F32))


def _split_bf16(x):
    hi = x.astype(BF16)
    lo = (x - hi.astype(F32)).astype(BF16)
    return hi, lo


def _wkv_body(r_ref, w_ref, k_ref, v_ref, a_ref, b_ref, s0_ref, y_ref, sout_ref, s_scr, *, tb, n_pairs):
    tblk = pl.program_id(1)
    rows = n_pairs * HEAD

    @pl.when(tblk == 0)
    def _():
        for p in range(n_pairs):
            s_scr[pl.ds(p * HEAD, HEAD), :] = jnp.concatenate([s0_ref[0, 2 * p], s0_ref[0, 2 * p + 1]], axis=-1)

    sub = lax.broadcasted_iota(jnp.int32, (rows, LANES), 0)
    lane = lax.broadcasted_iota(jnp.int32, (rows, LANES), 1)
    diag = (sub % HEAD) == (lane % HEAD)
    diag_bf16 = jnp.where(diag, 1.0, 0.0).astype(BF16)
    rr = lax.broadcasted_iota(jnp.int32, (LANES, LANES), 0)
    cc = lax.broadcasted_iota(jnp.int32, (LANES, LANES), 1)
    ones_bd = jnp.where((rr // HEAD) == (cc // HEAD), 1.0, 0.0).astype(BF16)

    def bcast(row):
        return jnp.concatenate(
            [jnp.broadcast_to(row[:, p * LANES:(p + 1) * LANES], (HEAD, LANES)) for p in range(n_pairs)], axis=0)

    def head_sum(x_hi, x_lo=None):
        out = jnp.dot(x_hi, ones_bd, preferred_element_type=F32)
        if x_lo is not None:
            out = out + jnp.dot(x_lo, ones_bd, preferred_element_type=F32)
        return out

    sub8 = lax.broadcasted_iota(jnp.int32, (SUBLANES, LANES), 0)

    def step8(t8, carry):
        t0 = pl.multiple_of(t8 * SUBLANES, SUBLANES)
        r8, w8, k8, v8, a8, b8 = (ref[0, pl.ds(t0, SUBLANES), :] for ref in
                                  (r_ref, w_ref, k_ref, v_ref, a_ref, b_ref))
        y_tiles = [jnp.zeros((SUBLANES, LANES), F32) for _ in range(n_pairs)]
        for j in range(SUBLANES):
            row = lambda x8: x8[j:j + 1, :]
            s = s_scr[...]
            sa = head_sum(*_split_bf16(s * bcast(row(a8))))
            v_hi, v_lo = _split_bf16(row(v8))
            v_col = head_sum(bcast(v_hi) * diag_bf16, bcast(v_lo) * diag_bf16)
            s = s * bcast(row(w8)) + sa * bcast(row(b8)) + v_col * bcast(row(k8))
            s_scr[...] = s
            y_col = head_sum((s * bcast(row(r8))).astype(BF16))
            y_diag = jnp.where(diag, y_col, 0.0)
            for p in range(n_pairs):
                y_row = jnp.sum(y_diag[p * HEAD:(p + 1) * HEAD], axis=0, keepdims=True)
                y_tiles[p] = jnp.where(sub8 == j, y_row, y_tiles[p])
        for p in range(n_pairs):
            y_ref[0, pl.ds(t0, SUBLANES), pl.ds(p * LANES, LANES)] = y_tiles[p]
        return carry

    lax.fori_loop(0, tb // SUBLANES, step8, 0)

    @pl.when(tblk == pl.num_programs(1) - 1)
    def _():
        for p in range(n_pairs):
            sp = s_scr[pl.ds(p * HEAD, HEAD), :]
            sout_ref[0, 2 * p] = sp[:, :HEAD]
            sout_ref[0, 2 * p + 1] = sp[:, HEAD:]


def _wkv7(r, w, k, v, a, b, s0, *, name="wkv7"):
    bsz, t, da = r.shape
    h = da // HEAD
    n_pairs = h // 2
    tb = _tile(t, 128, SUBLANES)
    seq = pl.BlockSpec((1, tb, da), lambda i, j: (i, j, 0))
    st = pl.BlockSpec((1, h, HEAD, HEAD), lambda i, j: (i, 0, 0, 0))
    return pl.pallas_call(
        functools.partial(_wkv_body, tb=tb, n_pairs=n_pairs),
        out_shape=(jax.ShapeDtypeStruct((bsz, t, da), F32), jax.ShapeDtypeStruct((bsz, h, HEAD, HEAD), F32)),
        grid=(bsz, t // tb),
        in_specs=[seq] * 6 + [st],
        out_specs=(seq, st),
        scratch_shapes=[pltpu.VMEM((n_pairs * HEAD, LANES), F32)],
        compiler_params=_cparams("parallel", "arbitrary"),
        name=name,
    )(r, w, k, v, a, b, s0)


def _sgu_body(u_ref, v_ref, ws_ref, bias_ref, o_ref, *, n_groups):
    for g in range(n_groups):
        cols = slice(g * GROUP_C, (g + 1) * GROUP_C)
        sv = jnp.dot(ws_ref[0, g], v_ref[:, cols].astype(BF16), preferred_element_type=F32)
        o_ref[:, cols] = (u_ref[:, cols] * (sv + bias_ref[0, :, cols])).astype(o_ref.dtype)


def _sgu(u, v, ws2, bias2, n_prompt_tiles, *, name="sgu"):
    rws, db = u.shape
    g = db // GROUP_C
    sel = lambda i: jnp.where(i < n_prompt_tiles, 0, 1)
    tile = pl.BlockSpec((CHUNK, db), lambda i: (i, 0))
    return pl.pallas_call(
        functools.partial(_sgu_body, n_groups=g),
        out_shape=jax.ShapeDtypeStruct((rws, db), BF16),
        grid=(rws // CHUNK,),
        in_specs=[tile, tile,
                  pl.BlockSpec((1, g, CHUNK, CHUNK), lambda i: (sel(i), 0, 0, 0)),
                  pl.BlockSpec((1, CHUNK, db), lambda i: (sel(i), 0, 0))],
        out_specs=tile,
        compiler_params=_cparams("parallel"),
        name=name,
    )(u, v, ws2, bias2)


def _rwkv_prep(z, v_first, layer, p, d_a):
    r, k, v = z[:, :d_a], z[:, d_a:2 * d_a], z[:, 2 * d_a:3 * d_a]
    xl = z[:, 3 * d_a:]
    pad_rows = lambda w, lo: jnp.zeros((LORA_PAD, w.shape[-1]), F32).at[lo:lo + w.shape[0]].set(w)
    w_lora = _mm(jnp.tanh(xl), pad_rows(p['w2'][layer], 0), name="lora_w")
    a_lora = _mm(xl, pad_rows(p['a2'][layer], R_W), name="lora_a")
    g = _mm(jax.nn.sigmoid(xl), pad_rows(p['g2'][layer], R_W + R_A), name="lora_g")
    w_log = -jax.nn.softplus(-(p['w0'][layer] + w_lora)) - 0.5
    decay = jnp.exp(-jnp.exp(w_log))
    if layer == 0:
        v_first = v
    else:
        vl = layer - 1
        v1 = jnp.zeros((d_a, LANES), F32).at[:, :R_V].set(p['v1'][vl])
        v2 = jnp.zeros((LANES, d_a), F32).at[:R_V].set(p['v2'][vl])
        mix = _mm(_mm(v, v1, name="lora_v1"), v2, name="lora_v2")
        v = v + (v_first - v) * jax.nn.sigmoid(p['v0'][vl] + mix)
    a = jax.nn.sigmoid(p['a0'][layer] + a_lora)
    heads = lambda t: t.reshape(t.shape[0], d_a // HEAD, HEAD)
    kk = heads(k * p['k_k'][layer])
    kk = (kk / jnp.maximum(jnp.sqrt(jnp.sum(kk * kk, axis=-1, keepdims=True)), 1e-12)).reshape(-1, d_a)
    k = k * (1 + (a - 1) * p['k_a'][layer])
    return (r, decay, k, v, -kk, kk * a), g, v_first


def _rwkv_post(y, r, k, v, g, layer, p, d_a):
    heads = lambda t: t.reshape(t.shape[0], d_a // HEAD, HEAD)
    yh = heads(y)
    mu = jnp.mean(yh, axis=-1, keepdims=True)
    var = jnp.mean(jnp.square(yh - mu), axis=-1, keepdims=True)
    yn = ((yh - mu) * lax.rsqrt(var + GN_EPS)).reshape(-1, d_a) * p['lnx_g'][layer] + p['lnx_b'][layer]
    bonus = jnp.sum(heads(r) * heads(k) * p['r_k'][layer], axis=-1, keepdims=True) * heads(v)
    return ((yn + bonus.reshape(-1, d_a)) * g).astype(BF16)


def _moe_route(logits, n_experts, tm):
    rws = logits.shape[0]
    top_logit, top_idx = lax.top_k(logits, TOP_K)
    top_w = jax.nn.softmax(top_logit, axis=-1)
    flat_e = top_idx.reshape(-1).astype(jnp.int32)
    n_slots = rws * TOP_K
    n_tiles = n_slots // tm + n_experts
    order = jnp.argsort(flat_e, stable=True).astype(jnp.int32)
    sorted_e = flat_e[order]
    counts = jnp.sum(jax.nn.one_hot(flat_e, n_experts, dtype=jnp.int32), axis=0)
    padded = ((counts + tm - 1) // tm) * tm
    ends_padded = jnp.cumsum(padded)
    starts_padded = ends_padded - padded
    starts = jnp.cumsum(counts) - counts
    rank = jnp.arange(n_slots, dtype=jnp.int32) - starts[sorted_e]
    dest_sorted = starts_padded[sorted_e] + rank
    dest = jnp.zeros((n_slots,), jnp.int32).at[order].set(dest_sorted)
    src_row = jnp.zeros((n_tiles * tm,), jnp.int32).at[dest_sorted].set(order // TOP_K)
    tile_start = jnp.arange(n_tiles, dtype=jnp.int32) * tm
    tile_expert = jnp.minimum(jnp.searchsorted(ends_padded, tile_start, side='right'),
                              n_experts - 1).astype(jnp.int32)
    n_valid = (ends_padded[-1] // tm).astype(jnp.int32).reshape(1)
    return top_w, dest.reshape(rws, TOP_K), src_row, tile_expert, n_valid


def kernel(x_prompt, x_sample, c_prompt, c_sample, state_wkv, state_shift, w_ada, b_ada, w_in, mu_shift, w0, w2, a0, a2, g2, v0, v1, v2, k_k, k_a, r_k, lnx_g, lnx_b, w_oa, sgu_ln_g, sgu_ln_b, w_s, b_s, w_ob, w_out, ln1_g, ln1_b, ffn_w1, ffn_w3, ffn_w2, w_router, moe_w1, moe_w3, moe_w2, ln2_g, ln2_b):
    p = dict(w0=w0, w2=w2, a0=a0, a2=a2, g2=g2, v0=v0, v1=v1, v2=v2, k_k=k_k, k_a=k_a, r_k=r_k,
             lnx_g=lnx_g, lnx_b=lnx_b)
    bp, tp, d = x_prompt.shape
    bs, ts, _ = x_sample.shape
    depth = w_in.shape[0]
    h_a = state_wkv.shape[2]
    d_a = h_a * HEAD
    d_b = w_ob.shape[1]
    p_a = 3 * d_a + R_W + R_A + R_G
    p_al = 3 * d_a + LORA_PAD
    n_experts = moe_w1.shape[1]
    rp, rs = bp * tp, bs * ts
    alpha = (2 * depth) ** 0.25
    assert tp % CHUNK == 0 and CHUNK % ts == 0 and rs % CHUNK == 0 and d_b % GROUP_C == 0

    def split(a):
        return a[:rp].reshape(bp, tp, -1), a[rp:].reshape(bs, ts, -1)

    def join(ap, as_):
        return jnp.concatenate([ap.reshape(rp, -1), as_.reshape(rs, -1)], axis=0)

    def per_batch(vec_p, vec_s):
        return join(jnp.broadcast_to(vec_p[:, None, :], (bp, tp, vec_p.shape[-1])),
                    jnp.broadcast_to(vec_s[:, None, :], (bs, ts, vec_s.shape[-1])))

    c_all = jnp.concatenate([c_prompt, c_sample], axis=0)
    n_c = bp + bs
    c_pad = jnp.zeros((-(-n_c // 16) * 16, d), F32).at[:n_c].set(jax.nn.silu(c_all))
    mods = [_mm(c_pad, w_ada, layer=l, bias=b_ada[l], tn=512, name="ada") for l in range(depth)]

    w_in_al = jnp.concatenate(
        [w_in[..., :p_a], jnp.zeros((depth, d, p_al - p_a), w_in.dtype), w_in[..., p_a:]], axis=-1).astype(BF16)
    pad_vec = lambda a: jnp.pad(a, [(0, 0)] * (a.ndim - 1) + [(0, p_al - p_a)])
    mu_al = pad_vec(mu_shift)

    causal = jnp.tril(jnp.ones((CHUNK, CHUNK), bool))
    causal_s = jnp.tril(jnp.ones((ts, ts), bool))
    eye_s = jnp.eye(CHUNK // ts, dtype=F32)

    x = jnp.concatenate([x_prompt.reshape(rp, d), x_sample.reshape(rs, d)], axis=0)
    wkv_p, wkv_s, shift_p, shift_s, chunk_v = [], [], [], [], []
    v_first = None
    zeros_wkv = jnp.zeros((bp, h_a, HEAD, HEAD), F32)
    for l in range(depth):
        mod = mods[l][:n_c]
        sh1, sc1, gt1, sh2, sc2, gt2 = (per_batch(m[:bp], m[bp:]) for m in jnp.split(mod, 6, axis=-1))

        h = (x * (1 + sc1) + sh1).astype(BF16)
        proj = _mm(h, w_in_al, layer=l, name="in_proj")
        pa = proj[:, :p_al]
        pa_p, pa_s = split(pa)
        prev_p = jnp.concatenate([jnp.zeros((bp, 1, p_al), F32), pa_p[:, :-1]], axis=1)
        prev_s = jnp.concatenate([pad_vec(state_shift[l])[:, None, :], pa_s[:, :-1]], axis=1)
        z = pa + mu_al[l] * (join(prev_p, prev_s) - pa)
        shift_p.append(pa_p[:, -1, :p_a])
        shift_s.append(pa_s[:, -1, :p_a])

        scan_in, g, v_first = _rwkv_prep(z, v_first, l, p, d_a)
        ins_p, ins_s = zip(*(split(t) for t in scan_in))
        y_p, s_p = _wkv7(*ins_p, zeros_wkv, name="wkv_prompt")
        y_s, s_s = _wkv7(*ins_s, state_wkv[l], name="wkv_sample")
        wkv_p.append(s_p)
        wkv_s.append(s_s)
        ya = _mm(_rwkv_post(join(y_p, y_s), scan_in[0], scan_in[2], scan_in[3], g, l, p, d_a),
                 w_oa, layer=l, name="out_a")

        zb = jax.nn.gelu(proj[:, p_al:p_al + 2 * d_b], approximate=False)
        u = zb[:, :d_b]
        vn = _layer_norm(zb[:, d_b:], sgu_ln_g[l], sgu_ln_b[l], LN_EPS, name="sgu_ln")
        chunk_v.append(vn[rp:].reshape(bs, ts, d_b))
        ws_p = jnp.where(causal[None], w_s[l], 0)
        ws_small = jnp.where(causal_s[None], w_s[l][:, :ts, :ts], 0)
        ws_s = jnp.einsum('ab,gts->gatbs', eye_s, ws_small).reshape(-1, CHUNK, CHUNK)
        ws2 = jnp.stack([ws_p, ws_s]).astype(BF16)
        bias_p = jnp.repeat(jnp.swapaxes(b_s[l], 0, 1), GROUP_C, axis=-1)
        bias_s = jnp.repeat(jnp.tile(jnp.swapaxes(b_s[l][:, :ts], 0, 1), (CHUNK // ts, 1)), GROUP_C, axis=-1)
        yb = _mm(_sgu(u, vn, ws2, jnp.stack([bias_p, bias_s]), rp // CHUNK), w_ob, layer=l, name="out_b")

        gate = jax.nn.sigmoid(proj[:, p_al + 2 * d_b:])
        merged = (gate[:, :d] * ya + gate[:, d:] * yb).astype(BF16)
        mix = _mm(merged, w_out, layer=l, name="out_mix")
        x = _layer_norm(alpha * x + gt1 * mix, ln1_g[l], ln1_b[l], LN_EPS, name="ln1")

        h2 = (x * (1 + sc2) + sh2).astype(BF16)
        i = l // 2
        if l % 2 == 0:
            f = _mm(_swiglu(h2, ffn_w1, ffn_w3, i), ffn_w2, layer=i, tn=256, nk=2, name="ffn_down")
        else:
            tm_moe = 512
            w_r = jnp.zeros((d, LANES), F32).at[:, :n_experts].set(w_router[i])
            logits = _mm(h2, w_r, name="router")[:, :n_experts]
            top_w, dest, src_row, tile_expert, n_valid = _moe_route(logits, n_experts, tm_moe)
            xs = jnp.take(h2, src_row, axis=0)
            hs = _moe_up(xs, moe_w1, moe_w3, i, tile_expert, n_valid, tm=tm_moe)
            ys = _moe_down(hs, moe_w2, i, tile_expert, n_valid, tm=tm_moe)
            f = (top_w[:, 0:1] * jnp.take(ys, dest[:, 0], axis=0)
                 + top_w[:, 1:2] * jnp.take(ys, dest[:, 1], axis=0))
        x = _layer_norm(alpha * x + gt2 * f, ln2_g[l], ln2_b[l], LN_EPS, name="ln2")

    y_prompt, y_sample = split(x)
    return (y_prompt, y_sample, jnp.stack(wkv_p), jnp.stack(shift_p), jnp.stack(wkv_s), jnp.stack(shift_s),
            jnp.stack(chunk_v))
```

```python
import functools

import jax
import jax.numpy as jnp
from jax import lax
from jax.experimental import pallas as pl
from jax.experimental.pallas import tpu as pltpu

F32 = jnp.float32
BF16 = jnp.bfloat16

LANES = 128
SUBLANES = 8
VMEM_LIMIT_BYTES = 56 * 1024 * 1024

HEAD = 64
CHUNK = 128
SLABS = CHUNK // SUBLANES
GROUP_C = 128
R_W, R_A, R_V, R_G = 96, 96, 64, 256
LORA_PAD = 512
TOP_K = 2
GN_EPS = 64e-5
LN_EPS = 1e-5


def _cparams(*sem):
    return pltpu.CompilerParams(dimension_semantics=sem, vmem_limit_bytes=VMEM_LIMIT_BYTES)


def _tile(n, target, mult):
    best = None
    for t in range(mult, min(n, target) + 1, mult):
        if n % t == 0:
            best = t
    return best if best is not None else n


def _slab_rows(ref):
    return jnp.concatenate(
        [jnp.broadcast_to(ref[s:s + 1, :], (SUBLANES, ref.shape[-1])) for s in range(SLABS)], axis=0)


def _split_bf16(x):
    hi = x.astype(BF16)
    lo = (x - hi.astype(F32)).astype(BF16)
    return hi, lo


def _ones_block_diag():
    rr = lax.broadcasted_iota(jnp.int32, (LANES, LANES), 0)
    cc = lax.broadcasted_iota(jnp.int32, (LANES, LANES), 1)
    return jnp.where((rr // HEAD) == (cc // HEAD), 1.0, 0.0).astype(BF16)


def _head_sum(x, ones_bd):
    outs = []
    for c in range(x.shape[-1] // LANES):
        hi, lo = _split_bf16(x[:, c * LANES:(c + 1) * LANES])
        outs.append(jnp.dot(hi, ones_bd, preferred_element_type=F32)
                    + jnp.dot(lo, ones_bd, preferred_element_type=F32))
    return jnp.concatenate(outs, axis=-1)


def _mm_body(*refs, has_bias):
    if has_bias:
        x_ref, w_ref, b_ref, o_ref = refs
    else:
        x_ref, w_ref, o_ref = refs
    acc = jnp.dot(x_ref[...].astype(BF16), w_ref[...].astype(BF16), preferred_element_type=F32)
    if has_bias:
        acc = acc + b_ref[...]
    o_ref[...] = acc.astype(o_ref.dtype)


def _mm(x, w, *, layer=None, bias=None, out_dtype=F32, tm=1024, tn=512, col0=0, n_out=None, name="mm"):
    m, kdim = x.shape
    n = n_out if n_out is not None else w.shape[-1]
    assert w.shape[-2] == kdim
    tm = _tile(m, tm, 16)
    tn = _tile(n, tn, LANES)
    assert col0 % tn == 0
    cb = col0 // tn
    if layer is None:
        w_spec = pl.BlockSpec((kdim, tn), lambda i, j: (0, cb + j))
    else:
        w_spec = pl.BlockSpec((None, kdim, tn), lambda i, j: (layer, 0, cb + j))
    in_specs = [pl.BlockSpec((tm, kdim), lambda i, j: (i, 0)), w_spec]
    args = [x, w]
    if bias is not None:
        in_specs.append(pl.BlockSpec((1, tn), lambda i, j: (0, j)))
        args.append(bias.reshape(1, n).astype(F32))
    return pl.pallas_call(
        functools.partial(_mm_body, has_bias=bias is not None),
        out_shape=jax.ShapeDtypeStruct((m, n), out_dtype),
        grid=(m // tm, n // tn),
        in_specs=in_specs,
        out_specs=pl.BlockSpec((tm, tn), lambda i, j: (i, j)),
        compiler_params=_cparams("parallel", "parallel"),
        name=name,
    )(*args)


def _mm_ksplit_body(x_ref, w_ref, o_ref):
    o_ref[...] = jnp.dot(x_ref[...], w_ref[...].astype(BF16), preferred_element_type=F32)


def _mm_ksplit(x, w, layer, *, nk=2, tm=1024, tn=256, name="mm_ksplit"):
    m, kdim = x.shape
    n = w.shape[-1]
    tk = kdim // nk
    assert kdim % nk == 0 and tk % LANES == 0
    tm = _tile(m, tm, 16)
    tn = _tile(n, tn, LANES)
    return pl.pallas_call(
        _mm_ksplit_body,
        out_shape=jax.ShapeDtypeStruct((nk, m, n), F32),
        grid=(nk, m // tm, n // tn),
        in_specs=[pl.BlockSpec((tm, tk), lambda k, i, j: (i, k)),
                  pl.BlockSpec((None, tk, tn), lambda k, i, j: (layer, k, j))],
        out_specs=pl.BlockSpec((None, tm, tn), lambda k, i, j: (k, i, j)),
        compiler_params=_cparams("parallel", "parallel", "parallel"),
        name=name,
    )(x, w)


def _swiglu_body(x_ref, w1_ref, w3_ref, o_ref):
    x = x_ref[...]
    a = jnp.dot(x, w1_ref[...].astype(BF16), preferred_element_type=F32)
    b = jnp.dot(x, w3_ref[...].astype(BF16), preferred_element_type=F32)
    o_ref[...] = (a * jax.nn.sigmoid(a) * b).astype(o_ref.dtype)


def _swiglu(x, w1, w3, layer, *, tm=1024, tn=256, name="ffn_up"):
    m, d = x.shape
    n = w1.shape[-1]
    tm = _tile(m, tm, 16)
    tn = _tile(n, tn, LANES)
    w_spec = pl.BlockSpec((None, d, tn), lambda i, j: (layer, 0, j))
    return pl.pallas_call(
        _swiglu_body,
        out_shape=jax.ShapeDtypeStruct((m, n), BF16),
        grid=(m // tm, n // tn),
        in_specs=[pl.BlockSpec((tm, d), lambda i, j: (i, 0)), w_spec, w_spec],
        out_specs=pl.BlockSpec((tm, tn), lambda i, j: (i, j)),
        compiler_params=_cparams("parallel", "parallel"),
        name=name,
    )(x, w1, w3)


def _expert_changed(te_ref, t):
    return jnp.logical_or(t == 0, te_ref[t] != te_ref[jnp.maximum(t - 1, 0)])


def _moe_up_body(te_ref, nv_ref, x_ref, w1_ref, w3_ref, o_ref, w1b_ref, w3b_ref):
    t = pl.program_id(1)

    @pl.when(_expert_changed(te_ref, t))
    def _():
        w1b_ref[...] = w1_ref[...].astype(BF16)
        w3b_ref[...] = w3_ref[...].astype(BF16)

    @pl.when(t < nv_ref[0])
    def _():
        x = x_ref[...]
        a = jnp.dot(x, w1b_ref[...], preferred_element_type=F32)
        b = jnp.dot(x, w3b_ref[...], preferred_element_type=F32)
        o_ref[...] = (a * jax.nn.sigmoid(a) * b).astype(o_ref.dtype)

    @pl.when(t >= nv_ref[0])
    def _():
        o_ref[...] = jnp.zeros_like(o_ref)


def _moe_up(xs, w1, w3, layer, tile_expert, n_valid, *, tm, tn=256, name="moe_up"):
    rs, d = xs.shape
    n = w1.shape[-1]
    tn = _tile(n, tn, LANES)
    w_spec = pl.BlockSpec((None, None, d, tn), lambda j, t, te, nv: (layer, te[t], 0, j))
    grid_spec = pltpu.PrefetchScalarGridSpec(
        num_scalar_prefetch=2,
        grid=(n // tn, rs // tm),
        in_specs=[pl.BlockSpec((tm, d), lambda j, t, te, nv: (t, 0)), w_spec, w_spec],
        out_specs=pl.BlockSpec((tm, tn), lambda j, t, te, nv: (t, j)),
        scratch_shapes=[pltpu.VMEM((d, tn), BF16), pltpu.VMEM((d, tn), BF16)],
    )
    return pl.pallas_call(
        _moe_up_body,
        out_shape=jax.ShapeDtypeStruct((rs, n), BF16),
        grid_spec=grid_spec,
        compiler_params=_cparams("arbitrary", "arbitrary"),
        name=name,
    )(tile_expert, n_valid, xs, w1, w3)


def _moe_down_body(te_ref, nv_ref, x_ref, w_ref, o_ref, wb_ref):
    t = pl.program_id(2)

    @pl.when(_expert_changed(te_ref, t))
    def _():
        wb_ref[...] = w_ref[...].astype(BF16)

    @pl.when(t < nv_ref[0])
    def _():
        o_ref[...] = jnp.dot(x_ref[...], wb_ref[...], preferred_element_type=F32)

    @pl.when(t >= nv_ref[0])
    def _():
        o_ref[...] = jnp.zeros_like(o_ref)


def _moe_down(hs, w2, layer, tile_expert, n_valid, *, tm, tn=512, nk=2, name="moe_down"):
    rs, f = hs.shape
    n = w2.shape[-1]
    tn = _tile(n, tn, LANES)
    tk = f // nk
    assert f % nk == 0 and tk % LANES == 0
    grid_spec = pltpu.PrefetchScalarGridSpec(
        num_scalar_prefetch=2,
        grid=(nk, n // tn, rs // tm),
        in_specs=[pl.BlockSpec((tm, tk), lambda k, j, t, te, nv: (t, k)),
                  pl.BlockSpec((None, None, tk, tn), lambda k, j, t, te, nv: (layer, te[t], k, j))],
        out_specs=pl.BlockSpec((None, tm, tn), lambda k, j, t, te, nv: (k, t, j)),
        scratch_shapes=[pltpu.VMEM((tk, tn), BF16)],
    )
    return pl.pallas_call(
        _moe_down_body,
        out_shape=jax.ShapeDtypeStruct((nk, rs, n), F32),
        grid_spec=grid_spec,
        compiler_params=_cparams("arbitrary", "arbitrary", "arbitrary"),
        name=name,
    )(tile_expert, n_valid, hs, w2)


def _slab_block_index(i, n_prompt_tiles, tiles_per_seq, n_prompt_blocks):
    return jnp.where(i < n_prompt_tiles, i // tiles_per_seq, n_prompt_blocks + i - n_prompt_tiles)


def _modulate_body(x_ref, sc_ref, sh_ref, h_ref):
    h_ref[...] = (x_ref[...] * (1.0 + _slab_rows(sc_ref)) + _slab_rows(sh_ref)).astype(h_ref.dtype)


def _res_ln_body(*refs, alpha, eps, n_parts, emit_h):
    x_ref, *f_refs = refs[:1 + n_parts]
    gt_ref, sc_ref, sh_ref, g_ref, b_ref = refs[1 + n_parts:6 + n_parts]
    outs = refs[6 + n_parts:]
    f = f_refs[0][...]
    for fr in f_refs[1:]:
        f = f + fr[...]
    y = alpha * x_ref[...] + _slab_rows(gt_ref) * f
    mu = jnp.mean(y, axis=-1, keepdims=True)
    yc = y - mu
    var = jnp.mean(yc * yc, axis=-1, keepdims=True)
    xn = yc * lax.rsqrt(var + eps) * g_ref[...] + b_ref[...]
    outs[0][...] = xn
    if emit_h:
        outs[1][...] = (xn * (1.0 + _slab_rows(sc_ref)) + _slab_rows(sh_ref)).astype(BF16)


class _Tiles:
    def __init__(self, bp, tp, bs, ts):
        self.rp, self.rs = bp * tp, bs * ts
        self.n_prompt = self.rp // CHUNK
        self.n_tiles = (self.rp + self.rs) // CHUNK
        self.per_seq = tp // CHUNK
        self.bp = bp

    def slab_spec(self, width, comp):
        idx = functools.partial(_slab_block_index, n_prompt_tiles=self.n_prompt, tiles_per_seq=self.per_seq,
                                n_prompt_blocks=self.bp)
        return pl.BlockSpec((SLABS, width), lambda i: (idx(i), comp))

    def row_spec(self, width, col_block=0):
        return pl.BlockSpec((CHUNK, width), lambda i: (i, col_block))


def _modulate(x, tab, comp_sc, comp_sh, tiles, *, name="modulate"):
    rws, d = x.shape
    return pl.pallas_call(
        _modulate_body,
        out_shape=jax.ShapeDtypeStruct((rws, d), BF16),
        grid=(tiles.n_tiles,),
        in_specs=[tiles.row_spec(d), tiles.slab_spec(d, comp_sc), tiles.slab_spec(d, comp_sh)],
        out_specs=tiles.row_spec(d),
        compiler_params=_cparams("parallel"),
        name=name,
    )(x, tab, tab)


def _res_ln(x, f_parts, tab_gt, comp_gt, tab_mod, comp_sc, comp_sh, g, b, alpha, tiles, *, emit_h, name):
    rws, d = x.shape
    vec = pl.BlockSpec((1, d), lambda i: (0, 0))
    f_specs, f_args = [], []
    for fp in f_parts:
        if fp.ndim == 3:
            for k in range(fp.shape[0]):
                f_specs.append(pl.BlockSpec((None, CHUNK, d), lambda i, k=k: (k, i, 0)))
                f_args.append(fp)
        else:
            f_specs.append(tiles.row_spec(d))
            f_args.append(fp)
    out_shape = [jax.ShapeDtypeStruct((rws, d), F32)]
    out_specs = [tiles.row_spec(d)]
    if emit_h:
        out_shape.append(jax.ShapeDtypeStruct((rws, d), BF16))
        out_specs.append(tiles.row_spec(d))
    outs = pl.pallas_call(
        functools.partial(_res_ln_body, alpha=alpha, eps=LN_EPS, n_parts=len(f_args), emit_h=emit_h),
        out_shape=out_shape,
        grid=(tiles.n_tiles,),
        in_specs=[tiles.row_spec(d)] + f_specs
                 + [tiles.slab_spec(d, comp_gt), tiles.slab_spec(d, comp_sc), tiles.slab_spec(d, comp_sh), vec, vec],
        out_specs=out_specs,
        compiler_params=_cparams("parallel"),
        name=name,
    )(x, *f_args, tab_gt, tab_mod, tab_mod, g.reshape(1, d), b.reshape(1, d))
    return outs if emit_h else (outs[0], None)


def _rwkv_prep_body(*refs, n_prompt_tiles, tiles_per_seq, first_layer):
    (r_ref, k_ref, v_ref, l_ref, pr_ref, pk_ref, pv_ref, pl_ref, sr_ref, sk_ref, sv_ref, sl_ref,
     mr_ref, mk_ref, mv_ref, ml_ref, w0_ref, a0_ref, kk_ref, ka_ref, w2_ref, a2_ref, g2_ref) = refs[:23]
    if first_layer:
        outs = refs[23:]
    else:
        v0_ref, v1_ref, v2_ref, vf_ref = refs[23:27]
        outs = refs[27:]
    ro_ref, wo_ref, ko_ref, vo_ref, ao_ref, bo_ref, go_ref = outs

    i = pl.program_id(0)
    is_sample = i >= n_prompt_tiles
    carry = jnp.logical_and(jnp.logical_not(is_sample), (i % tiles_per_seq) != 0)

    def shifted(x_ref, prev_ref, st_ref, mu_ref):
        x = x_ref[...]
        rowid = lax.broadcasted_iota(jnp.int32, x.shape, 0)
        prev = pltpu.roll(x, 1, 0)
        start_p = jnp.broadcast_to(jnp.where(carry, prev_ref[SUBLANES - 1:SUBLANES, :], 0.0), x.shape)
        start = jnp.where(is_sample, _slab_rows(st_ref), start_p)
        is_start = (rowid & jnp.where(is_sample, SUBLANES - 1, CHUNK - 1)) == 0
        prev = jnp.where(is_start, start, prev)
        return x + mu_ref[...] * (prev - x)

    ones_bd = _ones_block_diag()
    r = shifted(r_ref, pr_ref, sr_ref, mr_ref)
    k = shifted(k_ref, pk_ref, sk_ref, mk_ref)
    v = shifted(v_ref, pv_ref, sv_ref, mv_ref)
    xl = shifted(l_ref, pl_ref, sl_ref, ml_ref)

    w_lora = jnp.dot(jnp.tanh(xl).astype(BF16), w2_ref[...], preferred_element_type=F32)
    a_lora = jnp.dot(xl.astype(BF16), a2_ref[...], preferred_element_type=F32)
    go_ref[...] = jnp.dot(jax.nn.sigmoid(xl).astype(BF16), g2_ref[...], preferred_element_type=F32)

    z = -(w0_ref[...] + w_lora)
    softplus = jnp.maximum(z, 0.0) + jnp.log(1.0 + jnp.exp(-jnp.abs(z)))
    wo_ref[...] = jnp.exp(-jnp.exp(-softplus - 0.5))

    if not first_layer:
        mix = jnp.dot(jnp.dot(v.astype(BF16), v1_ref[...], preferred_element_type=F32).astype(BF16), v2_ref[...],
                      preferred_element_type=F32)
        v = v + (vf_ref[...] - v) * jax.nn.sigmoid(v0_ref[...] + mix)
    a = jax.nn.sigmoid(a0_ref[...] + a_lora)
    kk = k * kk_ref[...]
    kk = kk / jnp.maximum(jnp.sqrt(_head_sum(kk * kk, ones_bd)), 1e-12)
    ro_ref[...] = r
    ko_ref[...] = k * (1.0 + (a - 1.0) * ka_ref[...])
    vo_ref[...] = v
    ao_ref[...] = -kk
    bo_ref[...] = kk * a


def _rwkv_prep(proj, cols, shift_tab, mu_al, vecs, loras, vmix, tiles, d_a, *, name="rwkv_prep"):
    rws = proj.shape[0]
    first_layer = vmix is None
    c_r, c_k, c_v, c_l = cols
    assert c_r % d_a == 0 and c_k % d_a == 0 and c_v % d_a == 0 and c_l % LORA_PAD == 0
    assert (3 * d_a) % LORA_PAD == 0

    def prev_spec(width, cb):
        return pl.BlockSpec((SUBLANES, width), lambda i: (jnp.maximum(i * SLABS - 1, 0), cb))

    def start_spec(width, cb):
        return pl.BlockSpec((SLABS, width), lambda i: (jnp.maximum(i - tiles.n_prompt + 1, 0), cb))

    widths = (d_a, d_a, d_a, LORA_PAD)
    proj_cb = (c_r // d_a, c_k // d_a, c_v // d_a, c_l // LORA_PAD)
    tab_cb = (0, 1, 2, 3 * d_a // LORA_PAD)
    vec = lambda w: pl.BlockSpec((1, w), lambda i: (0, 0))
    full = lambda a: pl.BlockSpec(a.shape, lambda i: (0,) * a.ndim, pipeline_mode=pl.Buffered(1))
    in_specs = ([tiles.row_spec(w, cb) for w, cb in zip(widths, proj_cb)]
                + [prev_spec(w, cb) for w, cb in zip(widths, proj_cb)]
                + [start_spec(w, cb) for w, cb in zip(widths, tab_cb)]
                + [pl.BlockSpec((1, w), lambda i, cb=cb: (0, cb)) for w, cb in zip(widths, tab_cb)]
                + [vec(d_a)] * 4 + [full(w) for w in loras])
    args = [proj] * 8 + [shift_tab] * 4 + [mu_al] * 4 + list(vecs) + list(loras)
    if not first_layer:
        v0, v1p, v2p, v_first = vmix
        in_specs += [vec(d_a), full(v1p), full(v2p), tiles.row_spec(d_a)]
        args += [v0, v1p, v2p, v_first]
    out = jax.ShapeDtypeStruct((rws, d_a), F32)
    return pl.pallas_call(
        functools.partial(_rwkv_prep_body, n_prompt_tiles=tiles.n_prompt, tiles_per_seq=tiles.per_seq,
                          first_layer=first_layer),
        out_shape=[out] * 7,
        grid=(tiles.n_tiles,),
        in_specs=in_specs,
        out_specs=[tiles.row_spec(d_a)] * 7,
        compiler_params=_cparams("parallel"),
        name=name,
    )(*args)


def _rwkv_post_body(yp_ref, ys_ref, r_ref, k_ref, v_ref, g_ref, lg_ref, lb_ref, rk_ref, o_ref, *, n_prompt_tiles):
    ones_bd = _ones_block_diag()
    y = jnp.where(pl.program_id(0) < n_prompt_tiles, yp_ref[...], ys_ref[...])
    mu = _head_sum(y, ones_bd) * (1.0 / HEAD)
    yc = y - mu
    var = _head_sum(yc * yc, ones_bd) * (1.0 / HEAD)
    yn = yc * lax.rsqrt(var + GN_EPS) * lg_ref[...] + lb_ref[...]
    bonus = _head_sum(r_ref[...] * k_ref[...] * rk_ref[...], ones_bd) * v_ref[...]
    o_ref[...] = ((yn + bonus) * g_ref[...]).astype(o_ref.dtype)


def _rwkv_post(y_p, y_s, r, k, v, g, lnx_g, lnx_b, r_k, tiles, *, name="rwkv_post"):
    rws, d_a = r.shape
    vec = pl.BlockSpec((1, d_a), lambda i: (0, 0))
    n_p = tiles.n_prompt
    return pl.pallas_call(
        functools.partial(_rwkv_post_body, n_prompt_tiles=n_p),
        out_shape=jax.ShapeDtypeStruct((rws, d_a), BF16),
        grid=(tiles.n_tiles,),
        in_specs=[pl.BlockSpec((CHUNK, d_a), lambda i: (jnp.minimum(i, n_p - 1), 0)),
                  pl.BlockSpec((CHUNK, d_a), lambda i: (jnp.maximum(i - n_p, 0), 0))]
                 + [tiles.row_spec(d_a)] * 4 + [vec] * 3,
        out_specs=tiles.row_spec(d_a),
        compiler_params=_cparams("parallel"),
        name=name,
    )(y_p, y_s, r, k, v, g, lnx_g.reshape(1, d_a), lnx_b.reshape(1, d_a), r_k.reshape(1, d_a))


def _wkv_body(r_ref, w_ref, k_ref, v_ref, a_ref, b_ref, s0_ref, y_ref, sout_ref, s_scr, *, tb, n_pairs):
    tblk = pl.program_id(1)
    rows = n_pairs * HEAD

    @pl.when(tblk == 0)
    def _():
        for p in range(n_pairs):
            s_scr[pl.ds(p * HEAD, HEAD), :] = jnp.concatenate([s0_ref[0, 2 * p], s0_ref[0, 2 * p + 1]], axis=-1)

    sub = lax.broadcasted_iota(jnp.int32, (rows, LANES), 0)
    lane = lax.broadcasted_iota(jnp.int32, (rows, LANES), 1)
    diag_f32 = jnp.where((sub % HEAD) == (lane % HEAD), 1.0, 0.0)
    diag_bf16 = diag_f32.astype(BF16)
    ones_bd = _ones_block_diag()
    sub8 = lax.broadcasted_iota(jnp.int32, (SUBLANES, LANES), 0)

    def bcast(row):
        return jnp.concatenate(
            [jnp.broadcast_to(row[:, p * LANES:(p + 1) * LANES], (HEAD, LANES)) for p in range(n_pairs)], axis=0)

    def head_sum(x):
        return jnp.dot(x, ones_bd, preferred_element_type=F32)

    def step8(t8, carry):
        t0 = pl.multiple_of(t8 * SUBLANES, SUBLANES)
        r8, w8, k8, v8, a8, b8 = (ref[pl.ds(t0, SUBLANES), :] for ref in
                                  (r_ref, w_ref, k_ref, v_ref, a_ref, b_ref))
        y_tiles = [jnp.zeros((SUBLANES, LANES), F32) for _ in range(n_pairs)]
        for j in range(SUBLANES):
            row = lambda x8: x8[j:j + 1, :]
            s = s_scr[...]
            sa = head_sum((s * bcast(row(a8))).astype(BF16))
            v_col = head_sum(bcast(row(v8).astype(BF16)) * diag_bf16)
            s = s * bcast(row(w8)) + sa * bcast(row(b8)) + v_col * bcast(row(k8))
            s_scr[...] = s
            y_diag = head_sum((s * bcast(row(r8))).astype(BF16)) * diag_f32
            for p in range(n_pairs):
                y_row = jnp.sum(y_diag[p * HEAD:(p + 1) * HEAD], axis=0, keepdims=True)
                y_tiles[p] = jnp.where(sub8 == j, y_row, y_tiles[p])
        for p in range(n_pairs):
            y_ref[pl.ds(t0, SUBLANES), pl.ds(p * LANES, LANES)] = y_tiles[p]
        return carry

    lax.fori_loop(0, tb // SUBLANES, step8, 0)

    @pl.when(tblk == pl.num_programs(1) - 1)
    def _():
        for p in range(n_pairs):
            sp = s_scr[pl.ds(p * HEAD, HEAD), :]
            sout_ref[0, 2 * p] = sp[:, :HEAD]
            sout_ref[0, 2 * p + 1] = sp[:, HEAD:]


def _wkv7(ins, s0, row0, bsz, t, *, name="wkv7"):
    da = ins[0].shape[-1]
    h = da // HEAD
    n_pairs = h // 2
    tb = _tile(t, 128, SUBLANES)
    nt = t // tb
    assert row0 % tb == 0
    seq_in = pl.BlockSpec((tb, da), lambda i, j: (row0 // tb + i * nt + j, 0))
    seq_out = pl.BlockSpec((tb, da), lambda i, j: (i * nt + j, 0))
    st = pl.BlockSpec((1, h, HEAD, HEAD), lambda i, j: (i, 0, 0, 0))
    return pl.pallas_call(
        functools.partial(_wkv_body, tb=tb, n_pairs=n_pairs),
        out_shape=(jax.ShapeDtypeStruct((bsz * t, da), F32), jax.ShapeDtypeStruct((bsz, h, HEAD, HEAD), F32)),
        grid=(bsz, nt),
        in_specs=[seq_in] * 6 + [st],
        out_specs=(seq_out, st),
        scratch_shapes=[pltpu.VMEM((n_pairs * HEAD, LANES), F32)],
        compiler_params=_cparams("parallel", "arbitrary"),
        name=name,
    )(*ins, s0)


def _gelu(x):
    return 0.5 * x * (1.0 + lax.erf(x * 0.7071067811865476))


def _sgu_body(u_ref, v_ref, g_ref, b_ref, ws_ref, bias_ref, o_ref, vn_ref, *, n_groups):
    v = _gelu(v_ref[...])
    mu = jnp.mean(v, axis=-1, keepdims=True)
    vc = v - mu
    var = jnp.mean(vc * vc, axis=-1, keepdims=True)
    vn = vc * lax.rsqrt(var + LN_EPS) * g_ref[...] + b_ref[...]
    vn_ref[...] = vn
    for g in range(n_groups):
        cols = slice(g * GROUP_C, (g + 1) * GROUP_C)
        sv = jnp.dot(ws_ref[0, g], vn[:, cols].astype(BF16), preferred_element_type=F32)
        o_ref[:, cols] = (_gelu(u_ref[:, cols]) * (sv + bias_ref[0, :, cols])).astype(o_ref.dtype)


def _sgu(proj, c_u, c_v, ln_g, ln_b, ws2, bias2, tiles, d_b, *, name="sgu"):
    rws = proj.shape[0]
    g = d_b // GROUP_C
    n_p = tiles.n_prompt
    assert c_u % d_b == 0 and c_v % d_b == 0
    sel = lambda i: jnp.where(i < n_p, 0, 1)
    vec = pl.BlockSpec((1, d_b), lambda i: (0, 0))
    vn_spec = pl.BlockSpec((CHUNK, d_b), lambda i: (jnp.maximum(i - n_p, 0), 0))
    return pl.pallas_call(
        functools.partial(_sgu_body, n_groups=g),
        out_shape=(jax.ShapeDtypeStruct((rws, d_b), BF16), jax.ShapeDtypeStruct((tiles.rs, d_b), F32)),
        grid=(tiles.n_tiles,),
        in_specs=[tiles.row_spec(d_b, c_u // d_b), tiles.row_spec(d_b, c_v // d_b), vec, vec,
                  pl.BlockSpec((1, g, CHUNK, CHUNK), lambda i: (sel(i), 0, 0, 0)),
                  pl.BlockSpec((1, CHUNK, d_b), lambda i: (sel(i), 0, 0))],
        out_specs=(tiles.row_spec(d_b), vn_spec),
        compiler_params=_cparams("arbitrary"),
        name=name,
    )(proj, proj, ln_g.reshape(1, d_b), ln_b.reshape(1, d_b), ws2, bias2)


def _merge_body(xa_ref, xb_ref, wa_ref, wb_ref, ga_ref, gb_ref, o_ref):
    ya = jnp.dot(xa_ref[...], wa_ref[...].astype(BF16), preferred_element_type=F32)
    yb = jnp.dot(xb_ref[...], wb_ref[...].astype(BF16), preferred_element_type=F32)
    o_ref[...] = (jax.nn.sigmoid(ga_ref[...]) * ya + jax.nn.sigmoid(gb_ref[...]) * yb).astype(o_ref.dtype)


def _merge(xa, xb, w_oa, w_ob, layer, proj, c_ga, c_gb, *, tm=512, tn=512, name="merge"):
    m, ka = xa.shape
    kb = xb.shape[-1]
    n = w_oa.shape[-1]
    tm = _tile(m, tm, 16)
    tn = _tile(n, tn, LANES)
    assert c_ga % tn == 0 and c_gb % tn == 0
    ba, bb = c_ga // tn, c_gb // tn
    return pl.pallas_call(
        _merge_body,
        out_shape=jax.ShapeDtypeStruct((m, n), BF16),
        grid=(m // tm, n // tn),
        in_specs=[pl.BlockSpec((tm, ka), lambda i, j: (i, 0)), pl.BlockSpec((tm, kb), lambda i, j: (i, 0)),
                  pl.BlockSpec((None, ka, tn), lambda i, j: (layer, 0, j)),
                  pl.BlockSpec((None, kb, tn), lambda i, j: (layer, 0, j)),
                  pl.BlockSpec((tm, tn), lambda i, j: (i, ba + j)), pl.BlockSpec((tm, tn), lambda i, j: (i, bb + j))],
        out_specs=pl.BlockSpec((tm, tn), lambda i, j: (i, j)),
        compiler_params=_cparams("parallel", "parallel"),
        name=name,
    )(xa, xb, w_oa, w_ob, proj, proj)


def _moe_route(logits, n_experts, tm):
    rws = logits.shape[0]
    top_logit, top_idx = lax.top_k(logits, TOP_K)
    top_w = jax.nn.softmax(top_logit, axis=-1)
    flat_e = top_idx.reshape(-1).astype(jnp.int32)
    n_slots = rws * TOP_K
    n_tiles = n_slots // tm + n_experts
    order = jnp.argsort(flat_e, stable=True).astype(jnp.int32)
    sorted_e = flat_e[order]
    counts = jnp.sum(jax.nn.one_hot(flat_e, n_experts, dtype=jnp.int32), axis=0)
    padded = ((counts + tm - 1) // tm) * tm
    ends_padded = jnp.cumsum(padded)
    starts_padded = ends_padded - padded
    starts = jnp.cumsum(counts) - counts
    rank = jnp.arange(n_slots, dtype=jnp.int32) - starts[sorted_e]
    dest_sorted = starts_padded[sorted_e] + rank
    dest = jnp.zeros((n_slots,), jnp.int32).at[order].set(dest_sorted)
    src_row = jnp.zeros((n_tiles * tm,), jnp.int32).at[dest_sorted].set(order // TOP_K)
    tile_start = jnp.arange(n_tiles, dtype=jnp.int32) * tm
    tile_expert = jnp.minimum(jnp.searchsorted(ends_padded, tile_start, side='right'),
                              n_experts - 1).astype(jnp.int32)
    n_valid = (ends_padded[-1] // tm).astype(jnp.int32).reshape(1)
    return top_w, dest.reshape(rws, TOP_K), src_row, tile_expert, n_valid


def kernel(x_prompt, x_sample, c_prompt, c_sample, state_wkv, state_shift, w_ada, b_ada, w_in, mu_shift, w0, w2, a0, a2, g2, v0, v1, v2, k_k, k_a, r_k, lnx_g, lnx_b, w_oa, sgu_ln_g, sgu_ln_b, w_s, b_s, w_ob, w_out, ln1_g, ln1_b, ffn_w1, ffn_w3, ffn_w2, w_router, moe_w1, moe_w3, moe_w2, ln2_g, ln2_b):
    bp, tp, d = x_prompt.shape
    bs, ts, _ = x_sample.shape
    depth = w_in.shape[0]
    h_a = state_wkv.shape[2]
    d_a = h_a * HEAD
    d_b = w_ob.shape[1]
    p_a = 3 * d_a + R_W + R_A + R_G
    p_al = 3 * d_a + LORA_PAD
    n_experts = moe_w1.shape[1]
    rp, rs = bp * tp, bs * ts
    alpha = (2 * depth) ** 0.25
    assert tp % CHUNK == 0 and ts == SUBLANES and rs % CHUNK == 0 and d_b % GROUP_C == 0
    tiles = _Tiles(bp, tp, bs, ts)

    c_all = jnp.concatenate([c_prompt, c_sample], axis=0)
    n_c = bp + bs
    c_pad = jnp.zeros((-(-n_c // 16) * 16, d), F32).at[:n_c].set(jax.nn.silu(c_all))
    mods = [_mm(c_pad, w_ada, layer=l, bias=b_ada[l], tn=512, name="ada") for l in range(depth)]
    slab_tab = lambda a, n_lead: jnp.concatenate([jnp.repeat(a[:n_lead], SLABS, axis=0), a[n_lead:]], axis=0)
    mod_tabs = [slab_tab(m[:n_c], bp) for m in mods]
    SH1, SC1, GT1, SH2, SC2, GT2 = range(6)

    c_u, c_v, c_ga, c_gb = 0, d_b, 2 * d_b, 2 * d_b + d
    c_pa = 2 * d_b + 2 * d
    w_in_al = jnp.concatenate(
        [w_in[..., p_a:], w_in[..., :p_a], jnp.zeros((depth, d, p_al - p_a), w_in.dtype)], axis=-1).astype(BF16)
    pad_vec = lambda a: jnp.pad(a, [(0, 0)] * (a.ndim - 1) + [(0, p_al - p_a)])
    mu_al = pad_vec(mu_shift)

    causal = jnp.tril(jnp.ones((CHUNK, CHUNK), bool))
    causal_s = jnp.tril(jnp.ones((ts, ts), bool))
    eye_s = jnp.eye(CHUNK // ts, dtype=F32)
    seq_last_p = jnp.arange(bp) * tp + tp - 1
    seq_last_s = rp + jnp.arange(bs) * ts + ts - 1
    pad_rows = lambda w, lo: jnp.zeros((LORA_PAD, w.shape[-1]), F32).at[lo:lo + w.shape[0]].set(w).astype(BF16)

    x = jnp.concatenate([x_prompt.reshape(rp, d), x_sample.reshape(rs, d)], axis=0)
    h = _modulate(x, mod_tabs[0], SC1, SH1, tiles)
    wkv_p, wkv_s, shift_p, shift_s, chunk_v = [], [], [], [], []
    v_first = None
    zeros_wkv = jnp.zeros((bp, h_a, HEAD, HEAD), F32)
    for l in range(depth):
        proj = _mm(h, w_in_al, layer=l, name="in_proj")
        shift_p.append(proj[seq_last_p, c_pa:c_pa + p_a])
        shift_s.append(proj[seq_last_s, c_pa:c_pa + p_a])

        shift_tab = jnp.concatenate([jnp.zeros((SLABS, p_al), F32), pad_vec(state_shift[l])], axis=0)
        vecs = [a[l].reshape(1, d_a) for a in (w0, a0, k_k, k_a)]
        loras = [pad_rows(w2[l], 0), pad_rows(a2[l], R_W), pad_rows(g2[l], R_W + R_A)]
        vmix = None
        if l > 0:
            v1p = jnp.zeros((d_a, LANES), F32).at[:, :R_V].set(v1[l - 1]).astype(BF16)
            v2p = jnp.zeros((LANES, d_a), F32).at[:R_V].set(v2[l - 1]).astype(BF16)
            vmix = (v0[l - 1].reshape(1, d_a), v1p, v2p, v_first)
        r_s, decay, k_m, v_m, kk_neg, kk_a, g = _rwkv_prep(
            proj, (c_pa, c_pa + d_a, c_pa + 2 * d_a, c_pa + 3 * d_a), shift_tab, mu_al[l].reshape(1, p_al),
            vecs, loras, vmix, tiles, d_a)
        if l == 0:
            v_first = v_m
        scan_in = (r_s, decay, k_m, v_m, kk_neg, kk_a)
        y_p, s_p = _wkv7(scan_in, zeros_wkv, 0, bp, tp, name="wkv_prompt")
        y_s, s_s = _wkv7(scan_in, state_wkv[l], rp, bs, ts, name="wkv_sample")
        wkv_p.append(s_p)
        wkv_s.append(s_s)
        xa = _rwkv_post(y_p, y_s, r_s, k_m, v_m, g, lnx_g[l], lnx_b[l], r_k[l], tiles)

        ws_p = jnp.where(causal[None], w_s[l], 0)
        ws_small = jnp.where(causal_s[None], w_s[l][:, :ts, :ts], 0)
        ws_s = jnp.einsum('ab,gts->gatbs', eye_s, ws_small).reshape(-1, CHUNK, CHUNK)
        ws2 = jnp.stack([ws_p, ws_s]).astype(BF16)
        bias_p = jnp.repeat(jnp.swapaxes(b_s[l], 0, 1), GROUP_C, axis=-1)
        bias_s = jnp.repeat(jnp.tile(jnp.swapaxes(b_s[l][:, :ts], 0, 1), (CHUNK // ts, 1)), GROUP_C, axis=-1)
        xb, vn_s = _sgu(proj, c_u, c_v, sgu_ln_g[l], sgu_ln_b[l], ws2, jnp.stack([bias_p, bias_s]), tiles, d_b)
        chunk_v.append(vn_s.reshape(bs, ts, d_b))

        merged = _merge(xa, xb, w_oa, w_ob, l, proj, c_ga, c_gb)
        mix = _mm(merged, w_out, layer=l, name="out_mix")
        x, h2 = _res_ln(x, [mix], mod_tabs[l], GT1, mod_tabs[l], SC2, SH2, ln1_g[l], ln1_b[l], alpha, tiles,
                        emit_h=True, name="ln1")

        i = l // 2
        if l % 2 == 0:
            f_parts = [_mm_ksplit(_swiglu(h2, ffn_w1, ffn_w3, i), ffn_w2, i, name="ffn_down")]
        else:
            tm_moe = 512
            w_r = jnp.zeros((d, LANES), F32).at[:, :n_experts].set(w_router[i])
            logits = _mm(h2, w_r, name="router")[:, :n_experts]
            top_w, dest, src_row, tile_expert, n_valid = _moe_route(logits, n_experts, tm_moe)
            xs = jnp.take(h2, src_row, axis=0)
            hs = _moe_up(xs, moe_w1, moe_w3, i, tile_expert, n_valid, tm=tm_moe)
            ys = _moe_down(hs, moe_w2, i, tile_expert, n_valid, tm=tm_moe)
            ys = ys[0] + ys[1]
            f_parts = [top_w[:, 0:1] * jnp.take(ys, dest[:, 0], axis=0)
                       + top_w[:, 1:2] * jnp.take(ys, dest[:, 1], axis=0)]
        nxt = min(l + 1, depth - 1)
        x, h = _res_ln(x, f_parts, mod_tabs[l], GT2, mod_tabs[nxt], SC1, SH1, ln2_g[l], ln2_b[l], alpha, tiles,
                       emit_h=l + 1 < depth, name="ln2")

    return (x[:rp].reshape(bp, tp, d), x[rp:].reshape(bs, ts, d), jnp.stack(wkv_p), jnp.stack(shift_p),
            jnp.stack(wkv_s), jnp.stack(shift_s), jnp.stack(chunk_v))
```

```python
import functools

import jax
import jax.numpy as jnp
from jax import lax
from jax.experimental import pallas as pl
from jax.experimental.pallas import tpu as pltpu

F32 = jnp.float32
BF16 = jnp.bfloat16

LANES = 128
SUBLANES = 8
VMEM_LIMIT_BYTES = 56 * 1024 * 1024

HEAD = 64
CHUNK = 128
SLABS = CHUNK // SUBLANES
GROUP_C = 128
R_W, R_A, R_V, R_G = 96, 96, 64, 256
LORA_PAD = 512
TOP_K = 2
GN_EPS = 64e-5
LN_EPS = 1e-5


def _cparams(*sem):
    return pltpu.CompilerParams(dimension_semantics=sem, vmem_limit_bytes=VMEM_LIMIT_BYTES)


def _tile(n, target, mult):
    best = None
    for t in range(mult, min(n, target) + 1, mult):
        if n % t == 0:
            best = t
    return best if best is not None else n


def _slab_rows(ref):
    return jnp.concatenate(
        [jnp.broadcast_to(ref[s:s + 1, :], (SUBLANES, ref.shape[-1])) for s in range(SLABS)], axis=0)


def _split_bf16(x):
    hi = x.astype(BF16)
    lo = (x - hi.astype(F32)).astype(BF16)
    return hi, lo


def _ones_block_diag():
    rr = lax.broadcasted_iota(jnp.int32, (LANES, LANES), 0)
    cc = lax.broadcasted_iota(jnp.int32, (LANES, LANES), 1)
    return jnp.where((rr // HEAD) == (cc // HEAD), 1.0, 0.0).astype(BF16)


def _head_sum(x, ones_bd):
    outs = []
    for c in range(x.shape[-1] // LANES):
        hi, lo = _split_bf16(x[:, c * LANES:(c + 1) * LANES])
        outs.append(jnp.dot(hi, ones_bd, preferred_element_type=F32)
                    + jnp.dot(lo, ones_bd, preferred_element_type=F32))
    return jnp.concatenate(outs, axis=-1)


def _mm_body(*refs, has_bias):
    if has_bias:
        x_ref, w_ref, b_ref, o_ref = refs
    else:
        x_ref, w_ref, o_ref = refs
    acc = jnp.dot(x_ref[...].astype(BF16), w_ref[...].astype(BF16), preferred_element_type=F32)
    if has_bias:
        acc = acc + b_ref[...]
    o_ref[...] = acc.astype(o_ref.dtype)


def _mm(x, w, *, layer=None, bias=None, out_dtype=F32, tm=1024, tn=512, col0=0, n_out=None, name="mm"):
    m, kdim = x.shape
    n = n_out if n_out is not None else w.shape[-1]
    assert w.shape[-2] == kdim
    tm = _tile(m, tm, 16)
    tn = _tile(n, tn, LANES)
    assert col0 % tn == 0
    cb = col0 // tn
    if layer is None:
        w_spec = pl.BlockSpec((kdim, tn), lambda i, j: (0, cb + j))
    else:
        w_spec = pl.BlockSpec((None, kdim, tn), lambda i, j: (layer, 0, cb + j))
    in_specs = [pl.BlockSpec((tm, kdim), lambda i, j: (i, 0)), w_spec]
    args = [x, w]
    if bias is not None:
        in_specs.append(pl.BlockSpec((1, tn), lambda i, j: (0, j)))
        args.append(bias.reshape(1, n).astype(F32))
    return pl.pallas_call(
        functools.partial(_mm_body, has_bias=bias is not None),
        out_shape=jax.ShapeDtypeStruct((m, n), out_dtype),
        grid=(m // tm, n // tn),
        in_specs=in_specs,
        out_specs=pl.BlockSpec((tm, tn), lambda i, j: (i, j)),
        compiler_params=_cparams("parallel", "parallel"),
        name=name,
    )(*args)


def _mm_ksplit_body(x_ref, w_ref, o_ref):
    o_ref[...] = jnp.dot(x_ref[...], w_ref[...].astype(BF16), preferred_element_type=F32)


def _mm_ksplit(x, w, layer, *, nk=2, tm=1024, tn=256, name="mm_ksplit"):
    m, kdim = x.shape
    n = w.shape[-1]
    tk = kdim // nk
    assert kdim % nk == 0 and tk % LANES == 0
    tm = _tile(m, tm, 16)
    tn = _tile(n, tn, LANES)
    return pl.pallas_call(
        _mm_ksplit_body,
        out_shape=jax.ShapeDtypeStruct((nk, m, n), F32),
        grid=(nk, m // tm, n // tn),
        in_specs=[pl.BlockSpec((tm, tk), lambda k, i, j: (i, k)),
                  pl.BlockSpec((None, tk, tn), lambda k, i, j: (layer, k, j))],
        out_specs=pl.BlockSpec((None, tm, tn), lambda k, i, j: (k, i, j)),
        compiler_params=_cparams("parallel", "parallel", "parallel"),
        name=name,
    )(x, w)


def _swiglu_body(x_ref, w1_ref, w3_ref, o_ref):
    x = x_ref[...]
    a = jnp.dot(x, w1_ref[...].astype(BF16), preferred_element_type=F32)
    b = jnp.dot(x, w3_ref[...].astype(BF16), preferred_element_type=F32)
    o_ref[...] = (a * jax.nn.sigmoid(a) * b).astype(o_ref.dtype)


def _swiglu(x, w1, w3, layer, *, tm=1024, tn=256, name="ffn_up"):
    m, d = x.shape
    n = w1.shape[-1]
    tm = _tile(m, tm, 16)
    tn = _tile(n, tn, LANES)
    w_spec = pl.BlockSpec((None, d, tn), lambda i, j: (layer, 0, j))
    return pl.pallas_call(
        _swiglu_body,
        out_shape=jax.ShapeDtypeStruct((m, n), BF16),
        grid=(m // tm, n // tn),
        in_specs=[pl.BlockSpec((tm, d), lambda i, j: (i, 0)), w_spec, w_spec],
        out_specs=pl.BlockSpec((tm, tn), lambda i, j: (i, j)),
        compiler_params=_cparams("parallel", "parallel"),
        name=name,
    )(x, w1, w3)


def _expert_changed(te_ref, t):
    return jnp.logical_or(t == 0, te_ref[t] != te_ref[jnp.maximum(t - 1, 0)])


def _moe_up_body(te_ref, nv_ref, x_ref, w1_ref, w3_ref, o_ref, w1b_ref, w3b_ref):
    t = pl.program_id(1)

    @pl.when(_expert_changed(te_ref, t))
    def _():
        w1b_ref[...] = w1_ref[...].astype(BF16)
        w3b_ref[...] = w3_ref[...].astype(BF16)

    @pl.when(t < nv_ref[0])
    def _():
        x = x_ref[...]
        a = jnp.dot(x, w1b_ref[...], preferred_element_type=F32)
        b = jnp.dot(x, w3b_ref[...], preferred_element_type=F32)
        o_ref[...] = (a * jax.nn.sigmoid(a) * b).astype(o_ref.dtype)

    @pl.when(t >= nv_ref[0])
    def _():
        o_ref[...] = jnp.zeros_like(o_ref)


def _moe_up(xs, w1, w3, layer, tile_expert, n_valid, *, tm, tn=256, name="moe_up"):
    rs, d = xs.shape
    n = w1.shape[-1]
    tn = _tile(n, tn, LANES)
    w_spec = pl.BlockSpec((None, None, d, tn), lambda j, t, te, nv: (layer, te[t], 0, j))
    grid_spec = pltpu.PrefetchScalarGridSpec(
        num_scalar_prefetch=2,
        grid=(n // tn, rs // tm),
        in_specs=[pl.BlockSpec((tm, d), lambda j, t, te, nv: (jnp.minimum(t, nv[0] - 1), 0)), w_spec, w_spec],
        out_specs=pl.BlockSpec((tm, tn), lambda j, t, te, nv: (t, j)),
        scratch_shapes=[pltpu.VMEM((d, tn), BF16), pltpu.VMEM((d, tn), BF16)],
    )
    return pl.pallas_call(
        _moe_up_body,
        out_shape=jax.ShapeDtypeStruct((rs, n), BF16),
        grid_spec=grid_spec,
        compiler_params=_cparams("arbitrary", "arbitrary"),
        name=name,
    )(tile_expert, n_valid, xs, w1, w3)


def _moe_down_body(te_ref, nv_ref, x_ref, w_ref, o_ref, wb_ref):
    t = pl.program_id(2)

    @pl.when(_expert_changed(te_ref, t))
    def _():
        wb_ref[...] = w_ref[...].astype(BF16)

    @pl.when(t < nv_ref[0])
    def _():
        o_ref[...] = jnp.dot(x_ref[...], wb_ref[...], preferred_element_type=F32)

    @pl.when(t >= nv_ref[0])
    def _():
        o_ref[...] = jnp.zeros_like(o_ref)


def _moe_down(hs, w2, layer, tile_expert, n_valid, *, tm, tn=512, nk=2, name="moe_down"):
    rs, f = hs.shape
    n = w2.shape[-1]
    tn = _tile(n, tn, LANES)
    tk = f // nk
    assert f % nk == 0 and tk % LANES == 0
    grid_spec = pltpu.PrefetchScalarGridSpec(
        num_scalar_prefetch=2,
        grid=(nk, n // tn, rs // tm),
        in_specs=[pl.BlockSpec((tm, tk), lambda k, j, t, te, nv: (jnp.minimum(t, nv[0] - 1), k)),
                  pl.BlockSpec((None, None, tk, tn), lambda k, j, t, te, nv: (layer, te[t], k, j))],
        out_specs=pl.BlockSpec((None, tm, tn), lambda k, j, t, te, nv: (k, t, j)),
        scratch_shapes=[pltpu.VMEM((tk, tn), BF16)],
    )
    return pl.pallas_call(
        _moe_down_body,
        out_shape=jax.ShapeDtypeStruct((nk, rs, n), F32),
        grid_spec=grid_spec,
        compiler_params=_cparams("arbitrary", "arbitrary", "arbitrary"),
        name=name,
    )(tile_expert, n_valid, hs, w2)


def _slab_block_index(i, n_prompt_tiles, tiles_per_seq, n_prompt_blocks):
    return jnp.where(i < n_prompt_tiles, i // tiles_per_seq, n_prompt_blocks + i - n_prompt_tiles)


def _modulate_body(x_ref, sc_ref, sh_ref, h_ref):
    h_ref[...] = (x_ref[...] * (1.0 + _slab_rows(sc_ref)) + _slab_rows(sh_ref)).astype(h_ref.dtype)


def _res_ln_body(*refs, alpha, eps, n_parts, emit_h):
    x_ref, *f_refs = refs[:1 + n_parts]
    gt_ref, sc_ref, sh_ref, g_ref, b_ref = refs[1 + n_parts:6 + n_parts]
    outs = refs[6 + n_parts:]
    f = f_refs[0][...]
    for fr in f_refs[1:]:
        f = f + fr[...]
    y = alpha * x_ref[...] + _slab_rows(gt_ref) * f
    mu = jnp.mean(y, axis=-1, keepdims=True)
    yc = y - mu
    var = jnp.mean(yc * yc, axis=-1, keepdims=True)
    xn = yc * lax.rsqrt(var + eps) * g_ref[...] + b_ref[...]
    outs[0][...] = xn
    if emit_h:
        outs[1][...] = (xn * (1.0 + _slab_rows(sc_ref)) + _slab_rows(sh_ref)).astype(BF16)


class _Tiles:
    def __init__(self, bp, tp, bs, ts):
        self.rp, self.rs = bp * tp, bs * ts
        self.n_prompt = self.rp // CHUNK
        self.n_tiles = (self.rp + self.rs) // CHUNK
        self.per_seq = tp // CHUNK
        self.bp = bp

    def slab_spec(self, width, comp):
        idx = functools.partial(_slab_block_index, n_prompt_tiles=self.n_prompt, tiles_per_seq=self.per_seq,
                                n_prompt_blocks=self.bp)
        return pl.BlockSpec((SLABS, width), lambda i: (idx(i), comp))

    def row_spec(self, width, col_block=0):
        return pl.BlockSpec((CHUNK, width), lambda i: (i, col_block))


def _modulate(x, tab, comp_sc, comp_sh, tiles, *, name="modulate"):
    rws, d = x.shape
    return pl.pallas_call(
        _modulate_body,
        out_shape=jax.ShapeDtypeStruct((rws, d), BF16),
        grid=(tiles.n_tiles,),
        in_specs=[tiles.row_spec(d), tiles.slab_spec(d, comp_sc), tiles.slab_spec(d, comp_sh)],
        out_specs=tiles.row_spec(d),
        compiler_params=_cparams("parallel"),
        name=name,
    )(x, tab, tab)


def _res_ln(x, f_parts, tab_gt, comp_gt, tab_mod, comp_sc, comp_sh, g, b, alpha, tiles, *, emit_h, name):
    rws, d = x.shape
    vec = pl.BlockSpec((1, d), lambda i: (0, 0))
    f_specs, f_args = [], []
    for fp in f_parts:
        if fp.ndim == 3:
            for k in range(fp.shape[0]):
                f_specs.append(pl.BlockSpec((None, CHUNK, d), lambda i, k=k: (k, i, 0)))
                f_args.append(fp)
        else:
            f_specs.append(tiles.row_spec(d))
            f_args.append(fp)
    out_shape = [jax.ShapeDtypeStruct((rws, d), F32)]
    out_specs = [tiles.row_spec(d)]
    if emit_h:
        out_shape.append(jax.ShapeDtypeStruct((rws, d), BF16))
        out_specs.append(tiles.row_spec(d))
    outs = pl.pallas_call(
        functools.partial(_res_ln_body, alpha=alpha, eps=LN_EPS, n_parts=len(f_args), emit_h=emit_h),
        out_shape=out_shape,
        grid=(tiles.n_tiles,),
        in_specs=[tiles.row_spec(d)] + f_specs
                 + [tiles.slab_spec(d, comp_gt), tiles.slab_spec(d, comp_sc), tiles.slab_spec(d, comp_sh), vec, vec],
        out_specs=out_specs,
        compiler_params=_cparams("parallel"),
        name=name,
    )(x, *f_args, tab_gt, tab_mod, tab_mod, g.reshape(1, d), b.reshape(1, d))
    return outs if emit_h else (outs[0], None)


def _rwkv_prep_body(*refs, n_prompt_tiles, tiles_per_seq, first_layer):
    (r_ref, k_ref, v_ref, l_ref, pr_ref, pk_ref, pv_ref, pl_ref, sr_ref, sk_ref, sv_ref, sl_ref,
     mr_ref, mk_ref, mv_ref, ml_ref, w0_ref, a0_ref, kk_ref, ka_ref, w2_ref, a2_ref, g2_ref) = refs[:23]
    if first_layer:
        outs = refs[23:]
    else:
        v0_ref, v1_ref, v2_ref, vf_ref = refs[23:27]
        outs = refs[27:]
    ro_ref, wo_ref, ko_ref, vo_ref, ao_ref, bo_ref, go_ref = outs

    i = pl.program_id(0)
    is_sample = i >= n_prompt_tiles
    carry = jnp.logical_and(jnp.logical_not(is_sample), (i % tiles_per_seq) != 0)

    def shifted(x_ref, prev_ref, st_ref, mu_ref):
        x = x_ref[...]
        rowid = lax.broadcasted_iota(jnp.int32, x.shape, 0)
        prev = pltpu.roll(x, 1, 0)
        start_p = jnp.broadcast_to(jnp.where(carry, prev_ref[SUBLANES - 1:SUBLANES, :], 0.0), x.shape)
        start = jnp.where(is_sample, _slab_rows(st_ref), start_p)
        is_start = (rowid & jnp.where(is_sample, SUBLANES - 1, CHUNK - 1)) == 0
        prev = jnp.where(is_start, start, prev)
        return x + mu_ref[...] * (prev - x)

    ones_bd = _ones_block_diag()
    r = shifted(r_ref, pr_ref, sr_ref, mr_ref)
    k = shifted(k_ref, pk_ref, sk_ref, mk_ref)
    v = shifted(v_ref, pv_ref, sv_ref, mv_ref)
    xl = shifted(l_ref, pl_ref, sl_ref, ml_ref)

    w_lora = jnp.dot(jnp.tanh(xl).astype(BF16), w2_ref[...], preferred_element_type=F32)
    a_lora = jnp.dot(xl.astype(BF16), a2_ref[...], preferred_element_type=F32)
    go_ref[...] = jnp.dot(jax.nn.sigmoid(xl).astype(BF16), g2_ref[...], preferred_element_type=F32)

    z = -(w0_ref[...] + w_lora)
    softplus = jnp.maximum(z, 0.0) + jnp.log(1.0 + jnp.exp(-jnp.abs(z)))
    wo_ref[...] = jnp.exp(-jnp.exp(-softplus - 0.5))

    if not first_layer:
        mix = jnp.dot(jnp.dot(v.astype(BF16), v1_ref[...], preferred_element_type=F32).astype(BF16), v2_ref[...],
                      preferred_element_type=F32)
        v = v + (vf_ref[...] - v) * jax.nn.sigmoid(v0_ref[...] + mix)
    a = jax.nn.sigmoid(a0_ref[...] + a_lora)
    kk = k * kk_ref[...]
    kk = kk / jnp.maximum(jnp.sqrt(_head_sum(kk * kk, ones_bd)), 1e-12)
    ro_ref[...] = r
    ko_ref[...] = k * (1.0 + (a - 1.0) * ka_ref[...])
    vo_ref[...] = v
    ao_ref[...] = -kk
    bo_ref[...] = kk * a


def _rwkv_prep(proj, cols, shift_tab, mu_al, vecs, loras, vmix, tiles, d_a, *, name="rwkv_prep"):
    rws = proj.shape[0]
    first_layer = vmix is None
    c_r, c_k, c_v, c_l = cols
    assert c_r % d_a == 0 and c_k % d_a == 0 and c_v % d_a == 0 and c_l % LORA_PAD == 0
    assert (3 * d_a) % LORA_PAD == 0

    def prev_spec(width, cb):
        return pl.BlockSpec((SUBLANES, width), lambda i: (jnp.maximum(i * SLABS - 1, 0), cb))

    def start_spec(width, cb):
        return pl.BlockSpec((SLABS, width), lambda i: (jnp.maximum(i - tiles.n_prompt + 1, 0), cb))

    widths = (d_a, d_a, d_a, LORA_PAD)
    proj_cb = (c_r // d_a, c_k // d_a, c_v // d_a, c_l // LORA_PAD)
    tab_cb = (0, 1, 2, 3 * d_a // LORA_PAD)
    vec = lambda w: pl.BlockSpec((1, w), lambda i: (0, 0))
    full = lambda a: pl.BlockSpec(a.shape, lambda i: (0,) * a.ndim, pipeline_mode=pl.Buffered(1))
    in_specs = ([tiles.row_spec(w, cb) for w, cb in zip(widths, proj_cb)]
                + [prev_spec(w, cb) for w, cb in zip(widths, proj_cb)]
                + [start_spec(w, cb) for w, cb in zip(widths, tab_cb)]
                + [pl.BlockSpec((1, w), lambda i, cb=cb: (0, cb)) for w, cb in zip(widths, tab_cb)]
                + [vec(d_a)] * 4 + [full(w) for w in loras])
    args = [proj] * 8 + [shift_tab] * 4 + [mu_al] * 4 + list(vecs) + list(loras)
    if not first_layer:
        v0, v1p, v2p, v_first = vmix
        in_specs += [vec(d_a), full(v1p), full(v2p), tiles.row_spec(d_a)]
        args += [v0, v1p, v2p, v_first]
    out = jax.ShapeDtypeStruct((rws, d_a), F32)
    return pl.pallas_call(
        functools.partial(_rwkv_prep_body, n_prompt_tiles=tiles.n_prompt, tiles_per_seq=tiles.per_seq,
                          first_layer=first_layer),
        out_shape=[out] * 7,
        grid=(tiles.n_tiles,),
        in_specs=in_specs,
        out_specs=[tiles.row_spec(d_a)] * 7,
        compiler_params=_cparams("parallel"),
        name=name,
    )(*args)


def _rwkv_post_body(yp_ref, ys_ref, r_ref, k_ref, v_ref, g_ref, lg_ref, lb_ref, rk_ref, o_ref, *, n_prompt_tiles):
    ones_bd = _ones_block_diag()
    y = jnp.where(pl.program_id(0) < n_prompt_tiles, yp_ref[...], ys_ref[...])
    mu = _head_sum(y, ones_bd) * (1.0 / HEAD)
    yc = y - mu
    var = _head_sum(yc * yc, ones_bd) * (1.0 / HEAD)
    yn = yc * lax.rsqrt(var + GN_EPS) * lg_ref[...] + lb_ref[...]
    bonus = _head_sum(r_ref[...] * k_ref[...] * rk_ref[...], ones_bd) * v_ref[...]
    o_ref[...] = ((yn + bonus) * g_ref[...]).astype(o_ref.dtype)


def _rwkv_post(y_p, y_s, r, k, v, g, lnx_g, lnx_b, r_k, tiles, *, name="rwkv_post"):
    rws, d_a = r.shape
    vec = pl.BlockSpec((1, d_a), lambda i: (0, 0))
    n_p = tiles.n_prompt
    return pl.pallas_call(
        functools.partial(_rwkv_post_body, n_prompt_tiles=n_p),
        out_shape=jax.ShapeDtypeStruct((rws, d_a), BF16),
        grid=(tiles.n_tiles,),
        in_specs=[pl.BlockSpec((CHUNK, d_a), lambda i: (jnp.minimum(i, n_p - 1), 0)),
                  pl.BlockSpec((CHUNK, d_a), lambda i: (jnp.maximum(i - n_p, 0), 0))]
                 + [tiles.row_spec(d_a)] * 4 + [vec] * 3,
        out_specs=tiles.row_spec(d_a),
        compiler_params=_cparams("parallel"),
        name=name,
    )(y_p, y_s, r, k, v, g, lnx_g.reshape(1, d_a), lnx_b.reshape(1, d_a), r_k.reshape(1, d_a))


SEQS = 2


def _wkv_body(*refs, tb, n_pairs):
    in_refs = [refs[q * 6:(q + 1) * 6] for q in range(SEQS)]
    s0_ref, y_ref, sout_ref, s_scr = refs[SEQS * 6:]
    tblk = pl.program_id(1)
    rows = n_pairs * HEAD

    @pl.when(tblk == 0)
    def _():
        for q in range(SEQS):
            for p in range(n_pairs):
                s_scr[q, pl.ds(p * HEAD, HEAD), :] = jnp.concatenate(
                    [s0_ref[q, 2 * p], s0_ref[q, 2 * p + 1]], axis=-1)

    r2 = lax.broadcasted_iota(jnp.int32, (2 * LANES, 2 * LANES), 0)
    c2 = lax.broadcasted_iota(jnp.int32, (2 * LANES, 2 * LANES), 1)
    ones_bd2 = jnp.where((r2 // HEAD) == (c2 // HEAD), 1.0, 0.0).astype(BF16)
    rr = lax.broadcasted_iota(jnp.int32, (SUBLANES * LANES, LANES), 0)
    cc = lax.broadcasted_iota(jnp.int32, (SUBLANES * LANES, LANES), 1)
    step_sel = jnp.where(jnp.logical_and((rr % LANES) // HEAD == cc // HEAD, cc % HEAD == rr // LANES),
                         1.0, 0.0).astype(BF16)

    def bcast(row):
        return jnp.concatenate(
            [jnp.broadcast_to(row[:, p * LANES:(p + 1) * LANES], (HEAD, LANES)) for p in range(n_pairs)], axis=0)

    def head_sum(x):
        half = x.shape[0] // 2
        x2 = jnp.concatenate([x[:half], x[half:]], axis=1)
        o2 = jnp.dot(x2, ones_bd2, preferred_element_type=F32)
        return jnp.concatenate([o2[:, :LANES], o2[:, LANES:]], axis=0)

    sub = lax.broadcasted_iota(jnp.int32, (rows, LANES), 0)
    lane = lax.broadcasted_iota(jnp.int32, (rows, LANES), 1)
    diag_bf16 = jnp.where((sub % HEAD) == (lane % HEAD), 1.0, 0.0).astype(BF16)

    def step8(t8, carry):
        t0 = pl.multiple_of(t8 * SUBLANES, SUBLANES)
        tiles8 = [[ref[pl.ds(t0, SUBLANES), :] for ref in in_refs[q]] for q in range(SEQS)]
        sr = [[] for _ in range(SEQS)]
        for j in range(SUBLANES):
            for q in range(SEQS):
                r8, w8, k8, v8, a8, b8 = (x8[j:j + 1, :] for x8 in tiles8[q])
                s = s_scr[q]
                sa = head_sum((s * bcast(a8)).astype(BF16))
                v_col = head_sum(bcast(v8.astype(BF16)) * diag_bf16)
                s = s * bcast(w8) + sa * bcast(b8) + v_col * bcast(k8)
                s_scr[q] = s
                sr[q].append((s * bcast(r8)).astype(BF16))
        for q in range(SEQS):
            y_cols = jnp.dot(jnp.concatenate(sr[q], axis=1), step_sel, preferred_element_type=F32)
            for p in range(n_pairs):
                yt = y_cols[p * HEAD:(p + 1) * HEAD].T
                y_ref[q, pl.ds(t0, SUBLANES), pl.ds(p * LANES, LANES)] = jnp.concatenate(
                    [yt[:SUBLANES], yt[HEAD:HEAD + SUBLANES]], axis=1)
        return carry

    lax.fori_loop(0, tb // SUBLANES, step8, 0)

    @pl.when(tblk == pl.num_programs(1) - 1)
    def _():
        for q in range(SEQS):
            for p in range(n_pairs):
                sp = s_scr[q, pl.ds(p * HEAD, HEAD), :]
                sout_ref[q, 2 * p] = sp[:, :HEAD]
                sout_ref[q, 2 * p + 1] = sp[:, HEAD:]


def _wkv7(ins, s0, s0_layer, row0, bsz, t, *, name="wkv7"):
    da = ins[0].shape[-1]
    h = da // HEAD
    n_pairs = h // 2
    tb = _tile(t, 128, SUBLANES)
    nt = t // tb
    assert row0 % tb == 0 and bsz % SEQS == 0
    seq_in = [pl.BlockSpec((tb, da), lambda i, j, q=q: (row0 // tb + (SEQS * i + q) * nt + j, 0))
              for q in range(SEQS)]
    if s0_layer is None:
        st_in = pl.BlockSpec((SEQS, h, HEAD, HEAD), lambda i, j: (i, 0, 0, 0))
    else:
        st_in = pl.BlockSpec((None, SEQS, h, HEAD, HEAD), lambda i, j: (s0_layer, i, 0, 0, 0))
    st_out = pl.BlockSpec((SEQS, h, HEAD, HEAD), lambda i, j: (i, 0, 0, 0))
    return pl.pallas_call(
        functools.partial(_wkv_body, tb=tb, n_pairs=n_pairs),
        out_shape=(jax.ShapeDtypeStruct((bsz, t, da), F32), jax.ShapeDtypeStruct((bsz, h, HEAD, HEAD), F32)),
        grid=(bsz // SEQS, nt),
        in_specs=[seq_in[q] for q in range(SEQS) for _ in ins] + [st_in],
        out_specs=(pl.BlockSpec((SEQS, tb, da), lambda i, j: (i, j, 0)), st_out),
        scratch_shapes=[pltpu.VMEM((SEQS, n_pairs * HEAD, LANES), F32)],
        compiler_params=_cparams("parallel", "arbitrary"),
        name=name,
    )(*[a for q in range(SEQS) for a in ins], s0)


def _gelu(x):
    return 0.5 * x * (1.0 + lax.erf(x * 0.7071067811865476))


def _sgu_body(u_ref, v_ref, g_ref, b_ref, ws_ref, bias_ref, o_ref, vn_ref, *, n_groups):
    v = _gelu(v_ref[...])
    mu = jnp.mean(v, axis=-1, keepdims=True)
    vc = v - mu
    var = jnp.mean(vc * vc, axis=-1, keepdims=True)
    vn = vc * lax.rsqrt(var + LN_EPS) * g_ref[...] + b_ref[...]
    vn_ref[...] = vn
    for g in range(n_groups):
        cols = slice(g * GROUP_C, (g + 1) * GROUP_C)
        sv = jnp.dot(ws_ref[0, g], vn[:, cols].astype(BF16), preferred_element_type=F32)
        o_ref[:, cols] = (_gelu(u_ref[:, cols]) * (sv + bias_ref[0, :, cols])).astype(o_ref.dtype)


def _sgu(proj, c_u, c_v, ln_g, ln_b, ws2, bias2, tiles, d_b, *, name="sgu"):
    rws = proj.shape[0]
    g = d_b // GROUP_C
    n_p = tiles.n_prompt
    assert c_u % d_b == 0 and c_v % d_b == 0
    sel = lambda i: jnp.where(i < n_p, 0, 1)
    vec = pl.BlockSpec((1, d_b), lambda i: (0, 0))
    vn_spec = pl.BlockSpec((CHUNK, d_b), lambda i: (jnp.maximum(i - n_p, 0), 0))
    return pl.pallas_call(
        functools.partial(_sgu_body, n_groups=g),
        out_shape=(jax.ShapeDtypeStruct((rws, d_b), BF16), jax.ShapeDtypeStruct((tiles.rs, d_b), F32)),
        grid=(tiles.n_tiles,),
        in_specs=[tiles.row_spec(d_b, c_u // d_b), tiles.row_spec(d_b, c_v // d_b), vec, vec,
                  pl.BlockSpec((1, g, CHUNK, CHUNK), lambda i: (sel(i), 0, 0, 0)),
                  pl.BlockSpec((1, CHUNK, d_b), lambda i: (sel(i), 0, 0))],
        out_specs=(tiles.row_spec(d_b), vn_spec),
        compiler_params=_cparams("arbitrary"),
        name=name,
    )(proj, proj, ln_g.reshape(1, d_b), ln_b.reshape(1, d_b), ws2, bias2)


def _merge_body(xa_ref, xb_ref, wa_ref, wb_ref, ga_ref, gb_ref, o_ref):
    ya = jnp.dot(xa_ref[...], wa_ref[...].astype(BF16), preferred_element_type=F32)
    yb = jnp.dot(xb_ref[...], wb_ref[...].astype(BF16), preferred_element_type=F32)
    o_ref[...] = (jax.nn.sigmoid(ga_ref[...]) * ya + jax.nn.sigmoid(gb_ref[...]) * yb).astype(o_ref.dtype)


def _merge(xa, xb, w_oa, w_ob, layer, proj, c_ga, c_gb, *, tm=1024, tn=256, name="merge"):
    m, ka = xa.shape
    kb = xb.shape[-1]
    n = w_oa.shape[-1]
    tm = _tile(m, tm, 16)
    tn = _tile(n, tn, LANES)
    assert c_ga % tn == 0 and c_gb % tn == 0
    ba, bb = c_ga // tn, c_gb // tn
    return pl.pallas_call(
        _merge_body,
        out_shape=jax.ShapeDtypeStruct((m, n), BF16),
        grid=(m // tm, n // tn),
        in_specs=[pl.BlockSpec((tm, ka), lambda i, j: (i, 0)), pl.BlockSpec((tm, kb), lambda i, j: (i, 0)),
                  pl.BlockSpec((None, ka, tn), lambda i, j: (layer, 0, j)),
                  pl.BlockSpec((None, kb, tn), lambda i, j: (layer, 0, j)),
                  pl.BlockSpec((tm, tn), lambda i, j: (i, ba + j)), pl.BlockSpec((tm, tn), lambda i, j: (i, bb + j))],
        out_specs=pl.BlockSpec((tm, tn), lambda i, j: (i, j)),
        compiler_params=_cparams("parallel", "parallel"),
        name=name,
    )(xa, xb, w_oa, w_ob, proj, proj)


def _moe_route(logits, n_experts, tm):
    rws = logits.shape[0]
    top_logit, top_idx = lax.top_k(logits, TOP_K)
    top_w = jax.nn.softmax(top_logit, axis=-1)
    flat_e = top_idx.reshape(-1).astype(jnp.int32)
    n_slots = rws * TOP_K
    n_tiles = n_slots // tm + n_experts
    order = jnp.argsort(flat_e, stable=True).astype(jnp.int32)
    sorted_e = flat_e[order]
    counts = jnp.sum(jax.nn.one_hot(flat_e, n_experts, dtype=jnp.int32), axis=0)
    padded = ((counts + tm - 1) // tm) * tm
    ends_padded = jnp.cumsum(padded)
    starts_padded = ends_padded - padded
    starts = jnp.cumsum(counts) - counts
    rank = jnp.arange(n_slots, dtype=jnp.int32) - starts[sorted_e]
    dest_sorted = starts_padded[sorted_e] + rank
    dest = jnp.zeros((n_slots,), jnp.int32).at[order].set(dest_sorted)
    src_row = jnp.zeros((n_tiles * tm,), jnp.int32).at[dest_sorted].set(order // TOP_K)
    tile_start = jnp.arange(n_tiles, dtype=jnp.int32) * tm
    tile_expert = jnp.minimum(jnp.searchsorted(ends_padded, tile_start, side='right'),
                              n_experts - 1).astype(jnp.int32)
    n_valid = (ends_padded[-1] // tm).astype(jnp.int32).reshape(1)
    return top_w, dest.reshape(rws, TOP_K), src_row, tile_expert, n_valid


def kernel(x_prompt, x_sample, c_prompt, c_sample, state_wkv, state_shift, w_ada, b_ada, w_in, mu_shift, w0, w2, a0, a2, g2, v0, v1, v2, k_k, k_a, r_k, lnx_g, lnx_b, w_oa, sgu_ln_g, sgu_ln_b, w_s, b_s, w_ob, w_out, ln1_g, ln1_b, ffn_w1, ffn_w3, ffn_w2, w_router, moe_w1, moe_w3, moe_w2, ln2_g, ln2_b):
    bp, tp, d = x_prompt.shape
    bs, ts, _ = x_sample.shape
    depth = w_in.shape[0]
    h_a = state_wkv.shape[2]
    d_a = h_a * HEAD
    d_b = w_ob.shape[1]
    p_a = 3 * d_a + R_W + R_A + R_G
    p_al = 3 * d_a + LORA_PAD
    n_experts = moe_w1.shape[1]
    rp, rs = bp * tp, bs * ts
    alpha = (2 * depth) ** 0.25
    assert tp % CHUNK == 0 and ts == SUBLANES and rs % CHUNK == 0 and d_b % GROUP_C == 0
    tiles = _Tiles(bp, tp, bs, ts)

    c_all = jnp.concatenate([c_prompt, c_sample], axis=0)
    n_c = bp + bs
    c_pad = jnp.zeros((-(-n_c // 16) * 16, d), F32).at[:n_c].set(jax.nn.silu(c_all))
    mods = [_mm(c_pad, w_ada, layer=l, bias=b_ada[l], tn=512, name="ada") for l in range(depth)]
    slab_tab = lambda a, n_lead: jnp.concatenate([jnp.repeat(a[:n_lead], SLABS, axis=0), a[n_lead:]], axis=0)
    mod_tabs = [slab_tab(m[:n_c], bp) for m in mods]
    SH1, SC1, GT1, SH2, SC2, GT2 = range(6)

    c_u, c_v, c_ga, c_gb = 0, d_b, 2 * d_b, 2 * d_b + d
    c_pa = 2 * d_b + 2 * d
    w_in_al = jnp.concatenate(
        [w_in[..., p_a:], w_in[..., :p_a], jnp.zeros((depth, d, p_al - p_a), w_in.dtype)], axis=-1).astype(BF16)
    pad_vec = lambda a: jnp.pad(a, [(0, 0)] * (a.ndim - 1) + [(0, p_al - p_a)])
    mu_al = pad_vec(mu_shift)

    causal = jnp.tril(jnp.ones((CHUNK, CHUNK), bool))
    causal_s = jnp.tril(jnp.ones((ts, ts), bool))
    eye_s = jnp.eye(CHUNK // ts, dtype=F32)
    pad_rows = lambda w, lo: jnp.zeros((LORA_PAD, w.shape[-1]), F32).at[lo:lo + w.shape[0]].set(w).astype(BF16)

    x = jnp.concatenate([x_prompt.reshape(rp, d), x_sample.reshape(rs, d)], axis=0)
    h = _modulate(x, mod_tabs[0], SC1, SH1, tiles)
    wkv_p, wkv_s, shift_p, shift_s, chunk_v = [], [], [], [], []
    v_first = None
    zeros_wkv = jnp.zeros((bp, h_a, HEAD, HEAD), F32)
    for l in range(depth):
        proj = _mm(h, w_in_al, layer=l, name="in_proj")
        shift_p.append(proj[:rp].reshape(bp, tp, -1)[:, tp - 1, c_pa:c_pa + p_a])
        shift_s.append(proj[rp:].reshape(bs, ts, -1)[:, ts - 1, c_pa:c_pa + p_a])

        shift_tab = jnp.concatenate([jnp.zeros((SLABS, p_al), F32), pad_vec(state_shift[l])], axis=0)
        vecs = [a[l].reshape(1, d_a) for a in (w0, a0, k_k, k_a)]
        loras = [pad_rows(w2[l], 0), pad_rows(a2[l], R_W), pad_rows(g2[l], R_W + R_A)]
        vmix = None
        if l > 0:
            v1p = jnp.zeros((d_a, LANES), F32).at[:, :R_V].set(v1[l - 1]).astype(BF16)
            v2p = jnp.zeros((LANES, d_a), F32).at[:R_V].set(v2[l - 1]).astype(BF16)
            vmix = (v0[l - 1].reshape(1, d_a), v1p, v2p, v_first)
        r_s, decay, k_m, v_m, kk_neg, kk_a, g = _rwkv_prep(
            proj, (c_pa, c_pa + d_a, c_pa + 2 * d_a, c_pa + 3 * d_a), shift_tab, mu_al[l].reshape(1, p_al),
            vecs, loras, vmix, tiles, d_a)
        if l == 0:
            v_first = v_m
        scan_in = (r_s, decay, k_m, v_m, kk_neg, kk_a)
        y_p, s_p = _wkv7(scan_in, zeros_wkv, None, 0, bp, tp, name="wkv_prompt")
        y_s, s_s = _wkv7(scan_in, state_wkv, l, rp, bs, ts, name="wkv_sample")
        wkv_p.append(s_p)
        wkv_s.append(s_s)
        xa = _rwkv_post(y_p.reshape(rp, d_a), y_s.reshape(rs, d_a), r_s, k_m, v_m, g, lnx_g[l], lnx_b[l], r_k[l], tiles)

        ws_p = jnp.where(causal[None], w_s[l], 0)
        ws_small = jnp.where(causal_s[None], w_s[l][:, :ts, :ts], 0)
        ws_s = jnp.einsum('ab,gts->gatbs', eye_s, ws_small).reshape(-1, CHUNK, CHUNK)
        ws2 = jnp.stack([ws_p, ws_s]).astype(BF16)
        bias_p = jnp.repeat(jnp.swapaxes(b_s[l], 0, 1), GROUP_C, axis=-1)
        bias_s = jnp.repeat(jnp.tile(jnp.swapaxes(b_s[l][:, :ts], 0, 1), (CHUNK // ts, 1)), GROUP_C, axis=-1)
        xb, vn_s = _sgu(proj, c_u, c_v, sgu_ln_g[l], sgu_ln_b[l], ws2, jnp.stack([bias_p, bias_s]), tiles, d_b)
        chunk_v.append(vn_s.reshape(bs, ts, d_b))

        merged = _merge(xa, xb, w_oa, w_ob, l, proj, c_ga, c_gb)
        mix = _mm(merged, w_out, layer=l, name="out_mix")
        x, h2 = _res_ln(x, [mix], mod_tabs[l], GT1, mod_tabs[l], SC2, SH2, ln1_g[l], ln1_b[l], alpha, tiles,
                        emit_h=True, name="ln1")

        i = l // 2
        if l % 2 == 0:
            f_parts = [_mm_ksplit(_swiglu(h2, ffn_w1, ffn_w3, i), ffn_w2, i, name="ffn_down")]
        else:
            tm_up, tm_down = 1024, 512
            w_r = jnp.zeros((d, LANES), F32).at[:, :n_experts].set(w_router[i])
            logits = _mm(h2, w_r, name="router")[:, :n_experts]
            top_w, dest, src_row, tile_expert, n_valid = _moe_route(logits, n_experts, tm_up)
            xs = jnp.take(h2, src_row, axis=0)
            hs = _moe_up(xs, moe_w1, moe_w3, i, tile_expert, n_valid, tm=tm_up)
            ys = _moe_down(hs, moe_w2, i, jnp.repeat(tile_expert, tm_up // tm_down), n_valid * (tm_up // tm_down),
                           tm=tm_down)
            ys = ys[0] + ys[1]
            f_parts = [top_w[:, 0:1] * jnp.take(ys, dest[:, 0], axis=0)
                       + top_w[:, 1:2] * jnp.take(ys, dest[:, 1], axis=0)]
        nxt = min(l + 1, depth - 1)
        x, h = _res_ln(x, f_parts, mod_tabs[l], GT2, mod_tabs[nxt], SC1, SH1, ln2_g[l], ln2_b[l], alpha, tiles,
                       emit_h=l + 1 < depth, name="ln2")

    return (x[:rp].reshape(bp, tp, d), x[rp:].reshape(bs, ts, d), jnp.stack(wkv_p), jnp.stack(shift_p),
            jnp.stack(wkv_s), jnp.stack(shift_s), jnp.stack(chunk_v))
```

```python
import functools

import jax
import jax.numpy as jnp
from jax import lax
from jax.experimental import pallas as pl
from jax.experimental.pallas import tpu as pltpu

F32 = jnp.float32
BF16 = jnp.bfloat16

LANES = 128
SUBLANES = 8
VMEM_LIMIT_BYTES = 56 * 1024 * 1024

HEAD = 64
CHUNK = 128
SLABS = CHUNK // SUBLANES
GROUP_C = 128
R_W, R_A, R_V, R_G = 96, 96, 64, 256
LORA_PAD = 512
TOP_K = 2
GN_EPS = 64e-5
LN_EPS = 1e-5


def _cparams(*sem):
    return pltpu.CompilerParams(dimension_semantics=sem, vmem_limit_bytes=VMEM_LIMIT_BYTES)


def _tile(n, target, mult):
    best = None
    for t in range(mult, min(n, target) + 1, mult):
        if n % t == 0:
            best = t
    return best if best is not None else n


def _slab_rows(ref):
    return jnp.concatenate(
        [jnp.broadcast_to(ref[s:s + 1, :], (SUBLANES, ref.shape[-1])) for s in range(SLABS)], axis=0)


def _split_bf16(x):
    hi = x.astype(BF16)
    lo = (x - hi.astype(F32)).astype(BF16)
    return hi, lo


def _ones_block_diag():
    rr = lax.broadcasted_iota(jnp.int32, (LANES, LANES), 0)
    cc = lax.broadcasted_iota(jnp.int32, (LANES, LANES), 1)
    return jnp.where((rr // HEAD) == (cc // HEAD), 1.0, 0.0).astype(BF16)


def _head_sum(x, ones_bd):
    outs = []
    for c in range(x.shape[-1] // LANES):
        hi, lo = _split_bf16(x[:, c * LANES:(c + 1) * LANES])
        outs.append(jnp.dot(hi, ones_bd, preferred_element_type=F32)
                    + jnp.dot(lo, ones_bd, preferred_element_type=F32))
    return jnp.concatenate(outs, axis=-1)


def _mm_body(*refs, has_bias):
    if has_bias:
        x_ref, w_ref, b_ref, o_ref = refs
    else:
        x_ref, w_ref, o_ref = refs
    acc = jnp.dot(x_ref[...].astype(BF16), w_ref[...].astype(BF16), preferred_element_type=F32)
    if has_bias:
        acc = acc + b_ref[...]
    o_ref[...] = acc.astype(o_ref.dtype)


def _mm(x, w, *, layer=None, bias=None, out_dtype=F32, tm=1024, tn=512, col0=0, n_out=None, name="mm"):
    m, kdim = x.shape
    n = n_out if n_out is not None else w.shape[-1]
    assert w.shape[-2] == kdim
    tm = _tile(m, tm, 16)
    tn = _tile(n, tn, LANES)
    assert col0 % tn == 0
    cb = col0 // tn
    if layer is None:
        w_spec = pl.BlockSpec((kdim, tn), lambda i, j: (0, cb + j))
    else:
        w_spec = pl.BlockSpec((None, kdim, tn), lambda i, j: (layer, 0, cb + j))
    in_specs = [pl.BlockSpec((tm, kdim), lambda i, j: (i, 0)), w_spec]
    args = [x, w]
    if bias is not None:
        in_specs.append(pl.BlockSpec((1, tn), lambda i, j: (0, j)))
        args.append(bias.reshape(1, n).astype(F32))
    return pl.pallas_call(
        functools.partial(_mm_body, has_bias=bias is not None),
        out_shape=jax.ShapeDtypeStruct((m, n), out_dtype),
        grid=(m // tm, n // tn),
        in_specs=in_specs,
        out_specs=pl.BlockSpec((tm, tn), lambda i, j: (i, j)),
        compiler_params=_cparams("parallel", "parallel"),
        name=name,
    )(*args)


def _mm_ksplit_body(x_ref, w_ref, o_ref):
    o_ref[...] = jnp.dot(x_ref[...], w_ref[...].astype(BF16), preferred_element_type=F32)


def _mm_ksplit(x, w, layer, *, nk=2, tm=1024, tn=256, name="mm_ksplit"):
    m, kdim = x.shape
    n = w.shape[-1]
    tk = kdim // nk
    assert kdim % nk == 0 and tk % LANES == 0
    tm = _tile(m, tm, 16)
    tn = _tile(n, tn, LANES)
    return pl.pallas_call(
        _mm_ksplit_body,
        out_shape=jax.ShapeDtypeStruct((nk, m, n), F32),
        grid=(nk, m // tm, n // tn),
        in_specs=[pl.BlockSpec((tm, tk), lambda k, i, j: (i, k)),
                  pl.BlockSpec((None, tk, tn), lambda k, i, j: (layer, k, j))],
        out_specs=pl.BlockSpec((None, tm, tn), lambda k, i, j: (k, i, j)),
        compiler_params=_cparams("parallel", "parallel", "parallel"),
        name=name,
    )(x, w)


def _swiglu_body(x_ref, w1_ref, w3_ref, o_ref):
    x = x_ref[...]
    a = jnp.dot(x, w1_ref[...].astype(BF16), preferred_element_type=F32)
    b = jnp.dot(x, w3_ref[...].astype(BF16), preferred_element_type=F32)
    o_ref[...] = (a * jax.nn.sigmoid(a) * b).astype(o_ref.dtype)


def _swiglu(x, w1, w3, layer, *, tm=1024, tn=256, name="ffn_up"):
    m, d = x.shape
    n = w1.shape[-1]
    tm = _tile(m, tm, 16)
    tn = _tile(n, tn, LANES)
    w_spec = pl.BlockSpec((None, d, tn), lambda i, j: (layer, 0, j))
    return pl.pallas_call(
        _swiglu_body,
        out_shape=jax.ShapeDtypeStruct((m, n), BF16),
        grid=(m // tm, n // tn),
        in_specs=[pl.BlockSpec((tm, d), lambda i, j: (i, 0)), w_spec, w_spec],
        out_specs=pl.BlockSpec((tm, tn), lambda i, j: (i, j)),
        compiler_params=_cparams("parallel", "parallel"),
        name=name,
    )(x, w1, w3)


def _expert_changed(te_ref, t):
    return jnp.logical_or(t == 0, te_ref[t] != te_ref[jnp.maximum(t - 1, 0)])


def _moe_up_body(te_ref, nv_ref, x_ref, w1_ref, w3_ref, o_ref, w1b_ref, w3b_ref):
    t = pl.program_id(1)

    @pl.when(_expert_changed(te_ref, t))
    def _():
        w1b_ref[...] = w1_ref[...].astype(BF16)
        w3b_ref[...] = w3_ref[...].astype(BF16)

    @pl.when(t < nv_ref[0])
    def _():
        x = x_ref[...]
        a = jnp.dot(x, w1b_ref[...], preferred_element_type=F32)
        b = jnp.dot(x, w3b_ref[...], preferred_element_type=F32)
        o_ref[...] = (a * jax.nn.sigmoid(a) * b).astype(o_ref.dtype)

    @pl.when(t >= nv_ref[0])
    def _():
        o_ref[...] = jnp.zeros_like(o_ref)


def _moe_up(xs, w1, w3, layer, tile_expert, n_valid, *, tm, tn=256, name="moe_up"):
    rs, d = xs.shape
    n = w1.shape[-1]
    tn = _tile(n, tn, LANES)
    w_spec = pl.BlockSpec((None, None, d, tn), lambda j, t, te, nv: (layer, te[t], 0, j))
    grid_spec = pltpu.PrefetchScalarGridSpec(
        num_scalar_prefetch=2,
        grid=(n // tn, rs // tm),
        in_specs=[pl.BlockSpec((tm, d), lambda j, t, te, nv: (jnp.minimum(t, nv[0] - 1), 0)), w_spec, w_spec],
        out_specs=pl.BlockSpec((tm, tn), lambda j, t, te, nv: (t, j)),
        scratch_shapes=[pltpu.VMEM((d, tn), BF16), pltpu.VMEM((d, tn), BF16)],
    )
    return pl.pallas_call(
        _moe_up_body,
        out_shape=jax.ShapeDtypeStruct((rs, n), BF16),
        grid_spec=grid_spec,
        compiler_params=_cparams("arbitrary", "arbitrary"),
        name=name,
    )(tile_expert, n_valid, xs, w1, w3)


def _moe_down_body(te_ref, nv_ref, x_ref, w_ref, o_ref, wb_ref):
    t = pl.program_id(2)

    @pl.when(_expert_changed(te_ref, t))
    def _():
        wb_ref[...] = w_ref[...].astype(BF16)

    @pl.when(t < nv_ref[0])
    def _():
        o_ref[...] = jnp.dot(x_ref[...], wb_ref[...], preferred_element_type=F32)

    @pl.when(t >= nv_ref[0])
    def _():
        o_ref[...] = jnp.zeros_like(o_ref)


def _moe_down(hs, w2, layer, tile_expert, n_valid, *, tm, tn=512, nk=2, name="moe_down"):
    rs, f = hs.shape
    n = w2.shape[-1]
    tn = _tile(n, tn, LANES)
    tk = f // nk
    assert f % nk == 0 and tk % LANES == 0
    grid_spec = pltpu.PrefetchScalarGridSpec(
        num_scalar_prefetch=2,
        grid=(nk, n // tn, rs // tm),
        in_specs=[pl.BlockSpec((tm, tk), lambda k, j, t, te, nv: (jnp.minimum(t, nv[0] - 1), k)),
                  pl.BlockSpec((None, None, tk, tn), lambda k, j, t, te, nv: (layer, te[t], k, j))],
        out_specs=pl.BlockSpec((None, tm, tn), lambda k, j, t, te, nv: (k, t, j)),
        scratch_shapes=[pltpu.VMEM((tk, tn), BF16)],
    )
    return pl.pallas_call(
        _moe_down_body,
        out_shape=jax.ShapeDtypeStruct((nk, rs, n), F32),
        grid_spec=grid_spec,
        compiler_params=_cparams("arbitrary", "arbitrary", "arbitrary"),
        name=name,
    )(tile_expert, n_valid, hs, w2)


def _slab_block_index(i, n_prompt_tiles, tiles_per_seq, n_prompt_blocks):
    return jnp.where(i < n_prompt_tiles, i // tiles_per_seq, n_prompt_blocks + i - n_prompt_tiles)


def _read_x(x_refs, n_prompt_tiles, tile0):
    if len(x_refs) == 1:
        return x_refs[0][...]
    return jnp.where(pl.program_id(0) + tile0 < n_prompt_tiles, x_refs[0][...], x_refs[1][...])


def _modulate_body(*refs, n_x, n_prompt_tiles):
    sc_ref, sh_ref, h_ref = refs[n_x:]
    x = _read_x(refs[:n_x], n_prompt_tiles, 0)
    h_ref[...] = (x * (1.0 + _slab_rows(sc_ref)) + _slab_rows(sh_ref)).astype(h_ref.dtype)


def _res_ln_body(*refs, alpha, eps, n_x, n_parts, emit_h, n_prompt_tiles, tile0):
    f_refs = refs[n_x:n_x + n_parts]
    gt_ref, sc_ref, sh_ref, g_ref, b_ref = refs[n_x + n_parts:n_x + n_parts + 5]
    outs = refs[n_x + n_parts + 5:]
    f = f_refs[0][...]
    for fr in f_refs[1:]:
        f = f + fr[...]
    y = alpha * _read_x(refs[:n_x], n_prompt_tiles, tile0) + _slab_rows(gt_ref) * f
    mu = jnp.mean(y, axis=-1, keepdims=True)
    yc = y - mu
    var = jnp.mean(yc * yc, axis=-1, keepdims=True)
    xn = yc * lax.rsqrt(var + eps) * g_ref[...] + b_ref[...]
    outs[0][...] = xn
    if emit_h:
        outs[1][...] = (xn * (1.0 + _slab_rows(sc_ref)) + _slab_rows(sh_ref)).astype(BF16)


class _Tiles:
    def __init__(self, bp, tp, bs, ts):
        self.rp, self.rs = bp * tp, bs * ts
        self.n_prompt = self.rp // CHUNK
        self.n_tiles = (self.rp + self.rs) // CHUNK
        self.per_seq = tp // CHUNK
        self.bp = bp

    def slab_spec(self, width, comp, tile0=0):
        idx = functools.partial(_slab_block_index, n_prompt_tiles=self.n_prompt, tiles_per_seq=self.per_seq,
                                n_prompt_blocks=self.bp)
        return pl.BlockSpec((SLABS, width), lambda i: (idx(i + tile0), comp))

    def row_spec(self, width, col_block=0, tile0=0):
        return pl.BlockSpec((CHUNK, width), lambda i: (i + tile0, col_block))

    def x_specs(self, x, width, tile0=0):
        if not isinstance(x, tuple):
            return [self.row_spec(width, tile0=tile0)], [x]
        n_p = self.n_prompt
        return [pl.BlockSpec((CHUNK, width), lambda i: (jnp.minimum(i + tile0, n_p - 1), 0)),
                pl.BlockSpec((CHUNK, width), lambda i: (jnp.maximum(i + tile0 - n_p, 0), 0))], list(x)


def _modulate(x, tab, comp_sc, comp_sh, tiles, d, *, name="modulate"):
    x_specs, x_args = tiles.x_specs(x, d)
    return pl.pallas_call(
        functools.partial(_modulate_body, n_x=len(x_args), n_prompt_tiles=tiles.n_prompt),
        out_shape=jax.ShapeDtypeStruct((tiles.n_tiles * CHUNK, d), BF16),
        grid=(tiles.n_tiles,),
        in_specs=x_specs + [tiles.slab_spec(d, comp_sc), tiles.slab_spec(d, comp_sh)],
        out_specs=tiles.row_spec(d),
        compiler_params=_cparams("parallel"),
        name=name,
    )(*x_args, tab, tab)


def _res_ln(x, f_parts, tab_gt, comp_gt, tab_mod, comp_sc, comp_sh, g, b, alpha, tiles, d, *, emit_h, name,
            tile_range=None):
    tile0, n_run = tile_range if tile_range is not None else (0, tiles.n_tiles)
    vec = pl.BlockSpec((1, d), lambda i: (0, 0))
    x_specs, x_args = tiles.x_specs(x, d, tile0)
    f_specs, f_args = [], []
    for fp in f_parts:
        if fp.ndim == 3:
            for k in range(fp.shape[0]):
                f_specs.append(pl.BlockSpec((None, CHUNK, d), lambda i, k=k: (k, i + tile0, 0)))
                f_args.append(fp)
        else:
            f_specs.append(tiles.row_spec(d, tile0=tile0))
            f_args.append(fp)
    out_shape = [jax.ShapeDtypeStruct((n_run * CHUNK, d), F32)]
    out_specs = [tiles.row_spec(d)]
    if emit_h:
        out_shape.append(jax.ShapeDtypeStruct((n_run * CHUNK, d), BF16))
        out_specs.append(tiles.row_spec(d))
    slab = lambda comp: tiles.slab_spec(d, comp, tile0)
    outs = pl.pallas_call(
        functools.partial(_res_ln_body, alpha=alpha, eps=LN_EPS, n_x=len(x_args), n_parts=len(f_args),
                          emit_h=emit_h, n_prompt_tiles=tiles.n_prompt, tile0=tile0),
        out_shape=out_shape,
        grid=(n_run,),
        in_specs=x_specs + f_specs + [slab(comp_gt), slab(comp_sc), slab(comp_sh), vec, vec],
        out_specs=out_specs,
        compiler_params=_cparams("parallel"),
        name=name,
    )(*x_args, *f_args, tab_gt, tab_mod, tab_mod, g.reshape(1, d), b.reshape(1, d))
    return outs if emit_h else (outs[0], None)


def _rwkv_prep_body(*refs, n_prompt_tiles, tiles_per_seq, first_layer):
    (r_ref, k_ref, v_ref, l_ref, pr_ref, pk_ref, pv_ref, pl_ref, sr_ref, sk_ref, sv_ref, sl_ref,
     mr_ref, mk_ref, mv_ref, ml_ref, w0_ref, a0_ref, kk_ref, ka_ref, w2_ref, a2_ref, g2_ref) = refs[:23]
    if first_layer:
        outs = refs[23:]
    else:
        v0_ref, v1_ref, v2_ref, vf_ref = refs[23:27]
        outs = refs[27:]
    ro_ref, wo_ref, ko_ref, vo_ref, ao_ref, bo_ref, go_ref = outs

    i = pl.program_id(0)
    is_sample = i >= n_prompt_tiles
    carry = jnp.logical_and(jnp.logical_not(is_sample), (i % tiles_per_seq) != 0)

    def shifted(x_ref, prev_ref, st_ref, mu_ref):
        x = x_ref[...]
        rowid = lax.broadcasted_iota(jnp.int32, x.shape, 0)
        prev = pltpu.roll(x, 1, 0)
        start_p = jnp.broadcast_to(jnp.where(carry, prev_ref[SUBLANES - 1:SUBLANES, :], 0.0), x.shape)
        start = jnp.where(is_sample, _slab_rows(st_ref), start_p)
        is_start = (rowid & jnp.where(is_sample, SUBLANES - 1, CHUNK - 1)) == 0
        prev = jnp.where(is_start, start, prev)
        return x + mu_ref[...] * (prev - x)

    ones_bd = _ones_block_diag()
    r = shifted(r_ref, pr_ref, sr_ref, mr_ref)
    k = shifted(k_ref, pk_ref, sk_ref, mk_ref)
    v = shifted(v_ref, pv_ref, sv_ref, mv_ref)
    xl = shifted(l_ref, pl_ref, sl_ref, ml_ref)

    w_lora = jnp.dot(jnp.tanh(xl).astype(BF16), w2_ref[...], preferred_element_type=F32)
    a_lora = jnp.dot(xl.astype(BF16), a2_ref[...], preferred_element_type=F32)
    go_ref[...] = jnp.dot(jax.nn.sigmoid(xl).astype(BF16), g2_ref[...], preferred_element_type=F32)

    z = -(w0_ref[...] + w_lora)
    softplus = jnp.maximum(z, 0.0) + jnp.log(1.0 + jnp.exp(-jnp.abs(z)))
    wo_ref[...] = jnp.exp(-jnp.exp(-softplus - 0.5))

    if not first_layer:
        mix = jnp.dot(jnp.dot(v.astype(BF16), v1_ref[...], preferred_element_type=F32).astype(BF16), v2_ref[...],
                      preferred_element_type=F32)
        v = v + (vf_ref[...] - v) * jax.nn.sigmoid(v0_ref[...] + mix)
    a = jax.nn.sigmoid(a0_ref[...] + a_lora)
    kk = k * kk_ref[...]
    kk = kk / jnp.maximum(jnp.sqrt(_head_sum(kk * kk, ones_bd)), 1e-12)
    ro_ref[...] = r
    ko_ref[...] = k * (1.0 + (a - 1.0) * ka_ref[...])
    vo_ref[...] = v
    ao_ref[...] = -kk
    bo_ref[...] = kk * a


def _rwkv_prep(proj, cols, shift_tab, mu_al, vecs, loras, vmix, tiles, d_a, *, name="rwkv_prep"):
    rws = proj.shape[0]
    first_layer = vmix is None
    c_r, c_k, c_v, c_l = cols
    assert c_r % d_a == 0 and c_k % d_a == 0 and c_v % d_a == 0 and c_l % LORA_PAD == 0
    assert (3 * d_a) % LORA_PAD == 0

    def prev_spec(width, cb):
        return pl.BlockSpec((SUBLANES, width), lambda i: (jnp.maximum(i * SLABS - 1, 0), cb))

    def start_spec(width, cb):
        return pl.BlockSpec((SLABS, width), lambda i: (jnp.maximum(i - tiles.n_prompt + 1, 0), cb))

    widths = (d_a, d_a, d_a, LORA_PAD)
    proj_cb = (c_r // d_a, c_k // d_a, c_v // d_a, c_l // LORA_PAD)
    tab_cb = (0, 1, 2, 3 * d_a // LORA_PAD)
    vec = lambda w: pl.BlockSpec((1, w), lambda i: (0, 0))
    full = lambda a: pl.BlockSpec(a.shape, lambda i: (0,) * a.ndim, pipeline_mode=pl.Buffered(1))
    in_specs = ([tiles.row_spec(w, cb) for w, cb in zip(widths, proj_cb)]
                + [prev_spec(w, cb) for w, cb in zip(widths, proj_cb)]
                + [start_spec(w, cb) for w, cb in zip(widths, tab_cb)]
                + [pl.BlockSpec((1, w), lambda i, cb=cb: (0, cb)) for w, cb in zip(widths, tab_cb)]
                + [vec(d_a)] * 4 + [full(w) for w in loras])
    args = [proj] * 8 + [shift_tab] * 4 + [mu_al] * 4 + list(vecs) + list(loras)
    if not first_layer:
        v0, v1p, v2p, v_first = vmix
        in_specs += [vec(d_a), full(v1p), full(v2p), tiles.row_spec(d_a)]
        args += [v0, v1p, v2p, v_first]
    out = jax.ShapeDtypeStruct((rws, d_a), F32)
    return pl.pallas_call(
        functools.partial(_rwkv_prep_body, n_prompt_tiles=tiles.n_prompt, tiles_per_seq=tiles.per_seq,
                          first_layer=first_layer),
        out_shape=[out] * 7,
        grid=(tiles.n_tiles,),
        in_specs=in_specs,
        out_specs=[tiles.row_spec(d_a)] * 7,
        compiler_params=_cparams("parallel"),
        name=name,
    )(*args)


def _last_slab_body(r_ref, k_ref, v_ref, l_ref, o_ref):
    col = 0
    for x_ref in (r_ref, k_ref, v_ref, l_ref):
        o_ref[:, col:col + x_ref.shape[-1]] = x_ref[...]
        col += x_ref.shape[-1]


def _last_slabs(proj, cols, bp, tp, bs, d_a, *, name="last_slab"):
    rp = bp * tp
    widths = (d_a, d_a, d_a, LORA_PAD)
    slab = lambda s: jnp.where(s < bp, (s + 1) * (tp // SUBLANES) - 1, rp // SUBLANES + s - bp)
    in_specs = [pl.BlockSpec((SUBLANES, w), lambda s, cb=c // w: (slab(s), cb)) for w, c in zip(widths, cols)]
    p_al = 3 * d_a + LORA_PAD
    return pl.pallas_call(
        _last_slab_body,
        out_shape=jax.ShapeDtypeStruct(((bp + bs) * SUBLANES, p_al), F32),
        grid=(bp + bs,),
        in_specs=in_specs,
        out_specs=pl.BlockSpec((SUBLANES, p_al), lambda s: (s, 0)),
        compiler_params=_cparams("parallel"),
        name=name,
    )(proj, proj, proj, proj)


def _rwkv_post_body(yp_ref, ys_ref, r_ref, k_ref, v_ref, g_ref, lg_ref, lb_ref, rk_ref, o_ref, *, n_prompt_tiles):
    ones_bd = _ones_block_diag()
    y = jnp.where(pl.program_id(0) < n_prompt_tiles, yp_ref[...], ys_ref[...])
    mu = _head_sum(y, ones_bd) * (1.0 / HEAD)
    yc = y - mu
    var = _head_sum(yc * yc, ones_bd) * (1.0 / HEAD)
    yn = yc * lax.rsqrt(var + GN_EPS) * lg_ref[...] + lb_ref[...]
    bonus = _head_sum(r_ref[...] * k_ref[...] * rk_ref[...], ones_bd) * v_ref[...]
    o_ref[...] = ((yn + bonus) * g_ref[...]).astype(o_ref.dtype)


def _rwkv_post(y_p, y_s, r, k, v, g, lnx_g, lnx_b, r_k, tiles, *, name="rwkv_post"):
    rws, d_a = r.shape
    vec = pl.BlockSpec((1, d_a), lambda i: (0, 0))
    n_p = tiles.n_prompt
    return pl.pallas_call(
        functools.partial(_rwkv_post_body, n_prompt_tiles=n_p),
        out_shape=jax.ShapeDtypeStruct((rws, d_a), BF16),
        grid=(tiles.n_tiles,),
        in_specs=[pl.BlockSpec((CHUNK, d_a), lambda i: (jnp.minimum(i, n_p - 1), 0)),
                  pl.BlockSpec((CHUNK, d_a), lambda i: (jnp.maximum(i - n_p, 0), 0))]
                 + [tiles.row_spec(d_a)] * 4 + [vec] * 3,
        out_specs=tiles.row_spec(d_a),
        compiler_params=_cparams("parallel"),
        name=name,
    )(y_p, y_s, r, k, v, g, lnx_g.reshape(1, d_a), lnx_b.reshape(1, d_a), r_k.reshape(1, d_a))


SEQS = 2


def _wkv_body(*refs, tb, n_pairs):
    in_refs = [refs[q * 6:(q + 1) * 6] for q in range(SEQS)]
    s0_ref, y_ref, sout_ref, s_scr = refs[SEQS * 6:]
    tblk = pl.program_id(1)
    rows = n_pairs * HEAD

    @pl.when(tblk == 0)
    def _():
        for q in range(SEQS):
            for p in range(n_pairs):
                s_scr[q, pl.ds(p * HEAD, HEAD), :] = jnp.concatenate(
                    [s0_ref[q, 2 * p], s0_ref[q, 2 * p + 1]], axis=-1)

    r2 = lax.broadcasted_iota(jnp.int32, (2 * LANES, 2 * LANES), 0)
    c2 = lax.broadcasted_iota(jnp.int32, (2 * LANES, 2 * LANES), 1)
    ones_bd2 = jnp.where((r2 // HEAD) == (c2 // HEAD), 1.0, 0.0).astype(BF16)
    rr = lax.broadcasted_iota(jnp.int32, (SUBLANES * LANES, LANES), 0)
    cc = lax.broadcasted_iota(jnp.int32, (SUBLANES * LANES, LANES), 1)
    step_sel = jnp.where(jnp.logical_and((rr % LANES) // HEAD == cc // HEAD, cc % HEAD == rr // LANES),
                         1.0, 0.0).astype(BF16)

    def bcast(row):
        return jnp.concatenate(
            [jnp.broadcast_to(row[:, p * LANES:(p + 1) * LANES], (HEAD, LANES)) for p in range(n_pairs)], axis=0)

    def head_sum(x):
        half = x.shape[0] // 2
        x2 = jnp.concatenate([x[:half], x[half:]], axis=1)
        o2 = jnp.dot(x2, ones_bd2, preferred_element_type=F32)
        return jnp.concatenate([o2[:, :LANES], o2[:, LANES:]], axis=0)

    sub = lax.broadcasted_iota(jnp.int32, (rows, LANES), 0)
    lane = lax.broadcasted_iota(jnp.int32, (rows, LANES), 1)
    diag_bf16 = jnp.where((sub % HEAD) == (lane % HEAD), 1.0, 0.0).astype(BF16)

    def step8(t8, carry):
        t0 = pl.multiple_of(t8 * SUBLANES, SUBLANES)
        tiles8 = [[ref[pl.ds(t0, SUBLANES), :] for ref in in_refs[q]] for q in range(SEQS)]
        sr = [[] for _ in range(SEQS)]
        for j in range(SUBLANES):
            for q in range(SEQS):
                r8, w8, k8, v8, a8, b8 = (x8[j:j + 1, :] for x8 in tiles8[q])
                s = s_scr[q]
                sa = head_sum((s * bcast(a8)).astype(BF16))
                v_col = head_sum(bcast(v8.astype(BF16)) * diag_bf16)
                s = s * bcast(w8) + sa * bcast(b8) + v_col * bcast(k8)
                s_scr[q] = s
                sr[q].append((s * bcast(r8)).astype(BF16))
        for q in range(SEQS):
            y_cols = jnp.dot(jnp.concatenate(sr[q], axis=1), step_sel, preferred_element_type=F32)
            for p in range(n_pairs):
                yt = y_cols[p * HEAD:(p + 1) * HEAD].T
                y_ref[q, pl.ds(t0, SUBLANES), pl.ds(p * LANES, LANES)] = jnp.concatenate(
                    [yt[:SUBLANES], yt[HEAD:HEAD + SUBLANES]], axis=1)
        return carry

    lax.fori_loop(0, tb // SUBLANES, step8, 0)

    @pl.when(tblk == pl.num_programs(1) - 1)
    def _():
        for q in range(SEQS):
            for p in range(n_pairs):
                sp = s_scr[q, pl.ds(p * HEAD, HEAD), :]
                sout_ref[q, 2 * p] = sp[:, :HEAD]
                sout_ref[q, 2 * p + 1] = sp[:, HEAD:]


def _wkv7(ins, s0, s0_layer, row0, bsz, t, *, name="wkv7"):
    da = ins[0].shape[-1]
    h = da // HEAD
    n_pairs = h // 2
    tb = _tile(t, 128, SUBLANES)
    nt = t // tb
    assert row0 % tb == 0 and bsz % SEQS == 0
    seq_in = [pl.BlockSpec((tb, da), lambda i, j, q=q: (row0 // tb + (SEQS * i + q) * nt + j, 0))
              for q in range(SEQS)]
    if s0_layer is None:
        st_in = pl.BlockSpec((SEQS, h, HEAD, HEAD), lambda i, j: (i, 0, 0, 0))
    else:
        st_in = pl.BlockSpec((None, SEQS, h, HEAD, HEAD), lambda i, j: (s0_layer, i, 0, 0, 0))
    st_out = pl.BlockSpec((SEQS, h, HEAD, HEAD), lambda i, j: (i, 0, 0, 0))
    return pl.pallas_call(
        functools.partial(_wkv_body, tb=tb, n_pairs=n_pairs),
        out_shape=(jax.ShapeDtypeStruct((bsz, t, da), F32), jax.ShapeDtypeStruct((bsz, h, HEAD, HEAD), F32)),
        grid=(bsz // SEQS, nt),
        in_specs=[seq_in[q] for q in range(SEQS) for _ in ins] + [st_in],
        out_specs=(pl.BlockSpec((SEQS, tb, da), lambda i, j: (i, j, 0)), st_out),
        scratch_shapes=[pltpu.VMEM((SEQS, n_pairs * HEAD, LANES), F32)],
        compiler_params=_cparams("parallel", "arbitrary"),
        name=name,
    )(*[a for q in range(SEQS) for a in ins], s0)


def _gelu(x):
    return 0.5 * x * (1.0 + lax.erf(x * 0.7071067811865476))


def _sgu_body(u_ref, v_ref, g_ref, b_ref, ws_ref, bias_ref, o_ref, vn_ref, *, n_groups):
    v = _gelu(v_ref[...])
    mu = jnp.mean(v, axis=-1, keepdims=True)
    vc = v - mu
    var = jnp.mean(vc * vc, axis=-1, keepdims=True)
    vn = vc * lax.rsqrt(var + LN_EPS) * g_ref[...] + b_ref[...]
    vn_ref[...] = vn
    for g in range(n_groups):
        cols = slice(g * GROUP_C, (g + 1) * GROUP_C)
        sv = jnp.dot(ws_ref[0, g], vn[:, cols].astype(BF16), preferred_element_type=F32)
        o_ref[:, cols] = (_gelu(u_ref[:, cols]) * (sv + bias_ref[0, :, cols])).astype(o_ref.dtype)


def _sgu(proj, c_u, c_v, ln_g, ln_b, ws2, bias2, tiles, d_b, *, name="sgu"):
    rws = proj.shape[0]
    g = d_b // GROUP_C
    n_p = tiles.n_prompt
    assert c_u % d_b == 0 and c_v % d_b == 0
    sel = lambda i: jnp.where(i < n_p, 0, 1)
    vec = pl.BlockSpec((1, d_b), lambda i: (0, 0))
    vn_spec = pl.BlockSpec((CHUNK, d_b), lambda i: (jnp.maximum(i - n_p, 0), 0))
    return pl.pallas_call(
        functools.partial(_sgu_body, n_groups=g),
        out_shape=(jax.ShapeDtypeStruct((rws, d_b), BF16), jax.ShapeDtypeStruct((tiles.rs, d_b), F32)),
        grid=(tiles.n_tiles,),
        in_specs=[tiles.row_spec(d_b, c_u // d_b), tiles.row_spec(d_b, c_v // d_b), vec, vec,
                  pl.BlockSpec((1, g, CHUNK, CHUNK), lambda i: (sel(i), 0, 0, 0)),
                  pl.BlockSpec((1, CHUNK, d_b), lambda i: (sel(i), 0, 0))],
        out_specs=(tiles.row_spec(d_b), vn_spec),
        compiler_params=_cparams("arbitrary"),
        name=name,
    )(proj, proj, ln_g.reshape(1, d_b), ln_b.reshape(1, d_b), ws2, bias2)


def _merge_body(xa_ref, xb_ref, wa_ref, wb_ref, ga_ref, gb_ref, o_ref):
    ya = jnp.dot(xa_ref[...], wa_ref[...].astype(BF16), preferred_element_type=F32)
    yb = jnp.dot(xb_ref[...], wb_ref[...].astype(BF16), preferred_element_type=F32)
    o_ref[...] = (jax.nn.sigmoid(ga_ref[...]) * ya + jax.nn.sigmoid(gb_ref[...]) * yb).astype(o_ref.dtype)


def _merge(xa, xb, w_oa, w_ob, layer, proj, c_ga, c_gb, *, tm=1024, tn=256, name="merge"):
    m, ka = xa.shape
    kb = xb.shape[-1]
    n = w_oa.shape[-1]
    tm = _tile(m, tm, 16)
    tn = _tile(n, tn, LANES)
    assert c_ga % tn == 0 and c_gb % tn == 0
    ba, bb = c_ga // tn, c_gb // tn
    return pl.pallas_call(
        _merge_body,
        out_shape=jax.ShapeDtypeStruct((m, n), BF16),
        grid=(m // tm, n // tn),
        in_specs=[pl.BlockSpec((tm, ka), lambda i, j: (i, 0)), pl.BlockSpec((tm, kb), lambda i, j: (i, 0)),
                  pl.BlockSpec((None, ka, tn), lambda i, j: (layer, 0, j)),
                  pl.BlockSpec((None, kb, tn), lambda i, j: (layer, 0, j)),
                  pl.BlockSpec((tm, tn), lambda i, j: (i, ba + j)), pl.BlockSpec((tm, tn), lambda i, j: (i, bb + j))],
        out_specs=pl.BlockSpec((tm, tn), lambda i, j: (i, j)),
        compiler_params=_cparams("parallel", "parallel"),
        name=name,
    )(xa, xb, w_oa, w_ob, proj, proj)


def _moe_route(logits, n_experts, tm):
    rws = logits.shape[0]
    top_logit, top_idx = lax.top_k(logits, TOP_K)
    top_w = jax.nn.softmax(top_logit, axis=-1)
    flat_e = top_idx.reshape(-1).astype(jnp.int32)
    n_slots = rws * TOP_K
    n_tiles = n_slots // tm + n_experts
    order = jnp.argsort(flat_e, stable=True).astype(jnp.int32)
    sorted_e = flat_e[order]
    counts = jnp.sum(jax.nn.one_hot(flat_e, n_experts, dtype=jnp.int32), axis=0)
    padded = ((counts + tm - 1) // tm) * tm
    ends_padded = jnp.cumsum(padded)
    starts_padded = ends_padded - padded
    starts = jnp.cumsum(counts) - counts
    rank = jnp.arange(n_slots, dtype=jnp.int32) - starts[sorted_e]
    dest_sorted = starts_padded[sorted_e] + rank
    dest = jnp.zeros((n_slots,), jnp.int32).at[order].set(dest_sorted)
    src_row = jnp.zeros((n_tiles * tm,), jnp.int32).at[dest_sorted].set(order // TOP_K)
    tile_start = jnp.arange(n_tiles, dtype=jnp.int32) * tm
    tile_expert = jnp.minimum(jnp.searchsorted(ends_padded, tile_start, side='right'),
                              n_experts - 1).astype(jnp.int32)
    n_valid = (ends_padded[-1] // tm).astype(jnp.int32).reshape(1)
    return top_w, dest.reshape(rws, TOP_K), src_row, tile_expert, n_valid


def kernel(x_prompt, x_sample, c_prompt, c_sample, state_wkv, state_shift, w_ada, b_ada, w_in, mu_shift, w0, w2, a0, a2, g2, v0, v1, v2, k_k, k_a, r_k, lnx_g, lnx_b, w_oa, sgu_ln_g, sgu_ln_b, w_s, b_s, w_ob, w_out, ln1_g, ln1_b, ffn_w1, ffn_w3, ffn_w2, w_router, moe_w1, moe_w3, moe_w2, ln2_g, ln2_b):
    bp, tp, d = x_prompt.shape
    bs, ts, _ = x_sample.shape
    depth = w_in.shape[0]
    h_a = state_wkv.shape[2]
    d_a = h_a * HEAD
    d_b = w_ob.shape[1]
    p_a = 3 * d_a + R_W + R_A + R_G
    p_al = 3 * d_a + LORA_PAD
    n_experts = moe_w1.shape[1]
    rp, rs = bp * tp, bs * ts
    alpha = (2 * depth) ** 0.25
    assert tp % CHUNK == 0 and ts == SUBLANES and rs % CHUNK == 0 and d_b % GROUP_C == 0
    tiles = _Tiles(bp, tp, bs, ts)

    c_all = jnp.concatenate([c_prompt, c_sample], axis=0)
    n_c = bp + bs
    c_pad = jnp.zeros((-(-n_c // 16) * 16, d), F32).at[:n_c].set(jax.nn.silu(c_all))
    mods = [_mm(c_pad, w_ada, layer=l, bias=b_ada[l], tn=1024, name="ada") for l in range(depth)]
    slab_tab = lambda a, n_lead: jnp.concatenate([jnp.repeat(a[:n_lead], SLABS, axis=0), a[n_lead:]], axis=0)
    mod_tabs = [slab_tab(m[:n_c], bp) for m in mods]
    SH1, SC1, GT1, SH2, SC2, GT2 = range(6)

    c_u, c_v, c_ga, c_gb = 0, d_b, 2 * d_b, 2 * d_b + d
    c_pa = 2 * d_b + 2 * d
    w_in_al = jnp.concatenate(
        [w_in[..., p_a:], w_in[..., :p_a], jnp.zeros((depth, d, p_al - p_a), w_in.dtype)], axis=-1).astype(BF16)
    pad_vec = lambda a: jnp.pad(a, [(0, 0)] * (a.ndim - 1) + [(0, p_al - p_a)])
    mu_al = pad_vec(mu_shift)

    causal = jnp.tril(jnp.ones((CHUNK, CHUNK), bool))
    causal_s = jnp.tril(jnp.ones((ts, ts), bool))
    eye_s = jnp.eye(CHUNK // ts, dtype=F32)
    pad_rows = lambda w, lo: jnp.zeros((LORA_PAD, w.shape[-1]), F32).at[lo:lo + w.shape[0]].set(w).astype(BF16)

    x = (x_prompt.reshape(rp, d), x_sample.reshape(rs, d))
    h = _modulate(x, mod_tabs[0], SC1, SH1, tiles, d)
    wkv_p, wkv_s, shift_p, shift_s, chunk_v = [], [], [], [], []
    v_first = None
    zeros_wkv = jnp.zeros((bp, h_a, HEAD, HEAD), F32)
    for l in range(depth):
        proj = _mm(h, w_in_al, layer=l, name="in_proj")
        shift_tab = jnp.concatenate([jnp.zeros((SLABS, p_al), F32), pad_vec(state_shift[l])], axis=0)
        vecs = [a[l].reshape(1, d_a) for a in (w0, a0, k_k, k_a)]
        loras = [pad_rows(w2[l], 0), pad_rows(a2[l], R_W), pad_rows(g2[l], R_W + R_A)]
        vmix = None
        if l > 0:
            v1p = jnp.zeros((d_a, LANES), F32).at[:, :R_V].set(v1[l - 1]).astype(BF16)
            v2p = jnp.zeros((LANES, d_a), F32).at[:R_V].set(v2[l - 1]).astype(BF16)
            vmix = (v0[l - 1].reshape(1, d_a), v1p, v2p, v_first)
        pa_cols = (c_pa, c_pa + d_a, c_pa + 2 * d_a, c_pa + 3 * d_a)
        r_s, decay, k_m, v_m, kk_neg, kk_a, g = _rwkv_prep(
            proj, pa_cols, shift_tab, mu_al[l].reshape(1, p_al), vecs, loras, vmix, tiles, d_a)
        seq_last = _last_slabs(proj, pa_cols, bp, tp, bs, d_a).reshape(bp + bs, SUBLANES, p_al)[:, -1, :p_a]
        shift_p.append(seq_last[:bp])
        shift_s.append(seq_last[bp:])
        if l == 0:
            v_first = v_m
        scan_in = (r_s, decay, k_m, v_m, kk_neg, kk_a)
        y_p, s_p = _wkv7(scan_in, zeros_wkv, None, 0, bp, tp, name="wkv_prompt")
        y_s, s_s = _wkv7(scan_in, state_wkv, l, rp, bs, ts, name="wkv_sample")
        wkv_p.append(s_p)
        wkv_s.append(s_s)
        xa = _rwkv_post(y_p.reshape(rp, d_a), y_s.reshape(rs, d_a), r_s, k_m, v_m, g, lnx_g[l], lnx_b[l], r_k[l], tiles)

        ws_p = jnp.where(causal[None], w_s[l], 0)
        ws_small = jnp.where(causal_s[None], w_s[l][:, :ts, :ts], 0)
        ws_s = jnp.einsum('ab,gts->gatbs', eye_s, ws_small).reshape(-1, CHUNK, CHUNK)
        ws2 = jnp.stack([ws_p, ws_s]).astype(BF16)
        bias_p = jnp.repeat(jnp.swapaxes(b_s[l], 0, 1), GROUP_C, axis=-1)
        bias_s = jnp.repeat(jnp.tile(jnp.swapaxes(b_s[l][:, :ts], 0, 1), (CHUNK // ts, 1)), GROUP_C, axis=-1)
        xb, vn_s = _sgu(proj, c_u, c_v, sgu_ln_g[l], sgu_ln_b[l], ws2, jnp.stack([bias_p, bias_s]), tiles, d_b)
        chunk_v.append(vn_s.reshape(bs, ts, d_b))

        merged = _merge(xa, xb, w_oa, w_ob, l, proj, c_ga, c_gb)
        mix = _mm(merged, w_out, layer=l, name="out_mix")
        x, h2 = _res_ln(x, [mix], mod_tabs[l], GT1, mod_tabs[l], SC2, SH2, ln1_g[l], ln1_b[l], alpha, tiles, d,
                        emit_h=True, name="ln1")

        i = l // 2
        if l % 2 == 0:
            f_parts = [_mm_ksplit(_swiglu(h2, ffn_w1, ffn_w3, i), ffn_w2, i, name="ffn_down")]
        else:
            tm_up, tm_down = 1024, 512
            w_r = jnp.zeros((d, LANES), F32).at[:, :n_experts].set(w_router[i])
            logits = _mm(h2, w_r, name="router")[:, :n_experts]
            top_w, dest, src_row, tile_expert, n_valid = _moe_route(logits, n_experts, tm_up)
            xs = jnp.take(h2, src_row, axis=0)
            hs = _moe_up(xs, moe_w1, moe_w3, i, tile_expert, n_valid, tm=tm_up)
            ys = _moe_down(hs, moe_w2, i, jnp.repeat(tile_expert, tm_up // tm_down), n_valid * (tm_up // tm_down),
                           tm=tm_down)
            ys = ys[0] + ys[1]
            f_parts = [top_w[:, 0:1] * jnp.take(ys, dest[:, 0], axis=0)
                       + top_w[:, 1:2] * jnp.take(ys, dest[:, 1], axis=0)]
        ln2 = functools.partial(_res_ln, x, f_parts, mod_tabs[l], GT2, mod_tabs[min(l + 1, depth - 1)], SC1, SH1,
                                ln2_g[l], ln2_b[l], alpha, tiles, d)
        if l + 1 < depth:
            x, h = ln2(emit_h=True, name="ln2")
        else:
            y_prompt, _ = ln2(emit_h=False, name="ln2_prompt", tile_range=(0, tiles.n_prompt))
            y_sample, _ = ln2(emit_h=False, name="ln2_sample",
                              tile_range=(tiles.n_prompt, tiles.n_tiles - tiles.n_prompt))

    return (y_prompt.reshape(bp, tp, d), y_sample.reshape(bs, ts, d), jnp.stack(wkv_p), jnp.stack(shift_p),
            jnp.stack(wkv_s), jnp.stack(shift_s), jnp.stack(chunk_v))
```

```python
import functools

import jax
import jax.numpy as jnp
from jax import lax
from jax.experimental import pallas as pl
from jax.experimental.pallas import tpu as pltpu

F32 = jnp.float32
BF16 = jnp.bfloat16

LANES = 128
SUBLANES = 8
VMEM_LIMIT_BYTES = 56 * 1024 * 1024

HEAD = 64
CHUNK = 128
SLABS = CHUNK // SUBLANES
GROUP_C = 128
R_W, R_A, R_V, R_G = 96, 96, 64, 256
LORA_PAD = 512
TOP_K = 2
GN_EPS = 64e-5
LN_EPS = 1e-5


def _cparams(*sem):
    return pltpu.CompilerParams(dimension_semantics=sem, vmem_limit_bytes=VMEM_LIMIT_BYTES)


def _tile(n, target, mult):
    best = None
    for t in range(mult, min(n, target) + 1, mult):
        if n % t == 0:
            best = t
    return best if best is not None else n


def _slab_rows(ref):
    return jnp.concatenate(
        [jnp.broadcast_to(ref[s:s + 1, :], (SUBLANES, ref.shape[-1])) for s in range(SLABS)], axis=0)


def _split_bf16(x):
    hi = x.astype(BF16)
    lo = (x - hi.astype(F32)).astype(BF16)
    return hi, lo


def _ones_block_diag():
    rr = lax.broadcasted_iota(jnp.int32, (LANES, LANES), 0)
    cc = lax.broadcasted_iota(jnp.int32, (LANES, LANES), 1)
    return jnp.where((rr // HEAD) == (cc // HEAD), 1.0, 0.0).astype(BF16)


def _head_sum(x, ones_bd):
    outs = []
    for c in range(x.shape[-1] // LANES):
        hi, lo = _split_bf16(x[:, c * LANES:(c + 1) * LANES])
        outs.append(jnp.dot(hi, ones_bd, preferred_element_type=F32)
                    + jnp.dot(lo, ones_bd, preferred_element_type=F32))
    return jnp.concatenate(outs, axis=-1)


def _mm_body(*refs, has_bias):
    if has_bias:
        x_ref, w_ref, b_ref, o_ref = refs
    else:
        x_ref, w_ref, o_ref = refs
    acc = jnp.dot(x_ref[...].astype(BF16), w_ref[...].astype(BF16), preferred_element_type=F32)
    if has_bias:
        acc = acc + b_ref[...]
    o_ref[...] = acc.astype(o_ref.dtype)


def _mm(x, w, *, layer=None, bias=None, out_dtype=F32, tm=1024, tn=512, col0=0, n_out=None, name="mm"):
    m, kdim = x.shape
    n = n_out if n_out is not None else w.shape[-1]
    assert w.shape[-2] == kdim
    tm = _tile(m, tm, 16)
    tn = _tile(n, tn, LANES)
    assert col0 % tn == 0
    cb = col0 // tn
    if layer is None:
        w_spec = pl.BlockSpec((kdim, tn), lambda i, j: (0, cb + j))
    else:
        w_spec = pl.BlockSpec((None, kdim, tn), lambda i, j: (layer, 0, cb + j))
    in_specs = [pl.BlockSpec((tm, kdim), lambda i, j: (i, 0)), w_spec]
    args = [x, w]
    if bias is not None:
        in_specs.append(pl.BlockSpec((1, tn), lambda i, j: (0, j)))
        args.append(bias.reshape(1, n).astype(F32))
    return pl.pallas_call(
        functools.partial(_mm_body, has_bias=bias is not None),
        out_shape=jax.ShapeDtypeStruct((m, n), out_dtype),
        grid=(m // tm, n // tn),
        in_specs=in_specs,
        out_specs=pl.BlockSpec((tm, tn), lambda i, j: (i, j)),
        compiler_params=_cparams("parallel", "parallel"),
        name=name,
    )(*args)


def _mm_ksplit_body(x_ref, w_ref, o_ref):
    o_ref[...] = jnp.dot(x_ref[...], w_ref[...].astype(BF16), preferred_element_type=F32)


def _mm_ksplit(x, w, layer, *, nk=2, tm=1024, tn=256, name="mm_ksplit"):
    m, kdim = x.shape
    n = w.shape[-1]
    tk = kdim // nk
    assert kdim % nk == 0 and tk % LANES == 0
    tm = _tile(m, tm, 16)
    tn = _tile(n, tn, LANES)
    return pl.pallas_call(
        _mm_ksplit_body,
        out_shape=jax.ShapeDtypeStruct((nk, m, n), F32),
        grid=(nk, m // tm, n // tn),
        in_specs=[pl.BlockSpec((tm, tk), lambda k, i, j: (i, k)),
                  pl.BlockSpec((None, tk, tn), lambda k, i, j: (layer, k, j))],
        out_specs=pl.BlockSpec((None, tm, tn), lambda k, i, j: (k, i, j)),
        compiler_params=_cparams("parallel", "parallel", "parallel"),
        name=name,
    )(x, w)


def _swiglu_body(x_ref, w1_ref, w3_ref, o_ref):
    x = x_ref[...]
    a = jnp.dot(x, w1_ref[...].astype(BF16), preferred_element_type=F32)
    b = jnp.dot(x, w3_ref[...].astype(BF16), preferred_element_type=F32)
    o_ref[...] = (a * jax.nn.sigmoid(a) * b).astype(o_ref.dtype)


def _swiglu(x, w1, w3, layer, *, tm=1024, tn=256, name="ffn_up"):
    m, d = x.shape
    n = w1.shape[-1]
    tm = _tile(m, tm, 16)
    tn = _tile(n, tn, LANES)
    w_spec = pl.BlockSpec((None, d, tn), lambda i, j: (layer, 0, j))
    return pl.pallas_call(
        _swiglu_body,
        out_shape=jax.ShapeDtypeStruct((m, n), BF16),
        grid=(m // tm, n // tn),
        in_specs=[pl.BlockSpec((tm, d), lambda i, j: (i, 0)), w_spec, w_spec],
        out_specs=pl.BlockSpec((tm, tn), lambda i, j: (i, j)),
        compiler_params=_cparams("parallel", "parallel"),
        name=name,
    )(x, w1, w3)


def _expert_changed(te_ref, t):
    return jnp.logical_or(t == 0, te_ref[t] != te_ref[jnp.maximum(t - 1, 0)])


def _moe_up_body(te_ref, nv_ref, x_ref, w1_ref, w3_ref, o_ref, w1b_ref, w3b_ref):
    t = pl.program_id(1)

    @pl.when(_expert_changed(te_ref, t))
    def _():
        w1b_ref[...] = w1_ref[...].astype(BF16)
        w3b_ref[...] = w3_ref[...].astype(BF16)

    @pl.when(t < nv_ref[0])
    def _():
        x = x_ref[...]
        a = jnp.dot(x, w1b_ref[...], preferred_element_type=F32)
        b = jnp.dot(x, w3b_ref[...], preferred_element_type=F32)
        o_ref[...] = (a * jax.nn.sigmoid(a) * b).astype(o_ref.dtype)

    @pl.when(t >= nv_ref[0])
    def _():
        o_ref[...] = jnp.zeros_like(o_ref)


def _moe_up(xs, w1, w3, layer, tile_expert, n_valid, *, tm, tn=256, name="moe_up"):
    rs, d = xs.shape
    n = w1.shape[-1]
    tn = _tile(n, tn, LANES)
    w_spec = pl.BlockSpec((None, None, d, tn), lambda j, t, te, nv: (layer, te[t], 0, j))
    grid_spec = pltpu.PrefetchScalarGridSpec(
        num_scalar_prefetch=2,
        grid=(n // tn, rs // tm),
        in_specs=[pl.BlockSpec((tm, d), lambda j, t, te, nv: (jnp.minimum(t, nv[0] - 1), 0)), w_spec, w_spec],
        out_specs=pl.BlockSpec((tm, tn), lambda j, t, te, nv: (t, j)),
        scratch_shapes=[pltpu.VMEM((d, tn), BF16), pltpu.VMEM((d, tn), BF16)],
    )
    return pl.pallas_call(
        _moe_up_body,
        out_shape=jax.ShapeDtypeStruct((rs, n), BF16),
        grid_spec=grid_spec,
        compiler_params=_cparams("arbitrary", "arbitrary"),
        name=name,
    )(tile_expert, n_valid, xs, w1, w3)


def _moe_down_body(te_ref, nv_ref, x_ref, w_ref, o_ref, wb_ref):
    t = pl.program_id(2)

    @pl.when(_expert_changed(te_ref, t))
    def _():
        wb_ref[...] = w_ref[...].astype(BF16)

    @pl.when(t < nv_ref[0])
    def _():
        o_ref[...] = jnp.dot(x_ref[...], wb_ref[...], preferred_element_type=F32)

    @pl.when(t >= nv_ref[0])
    def _():
        o_ref[...] = jnp.zeros_like(o_ref)


def _moe_down(hs, w2, layer, tile_expert, n_valid, *, tm, tn=512, nk=2, name="moe_down"):
    rs, f = hs.shape
    n = w2.shape[-1]
    tn = _tile(n, tn, LANES)
    tk = f // nk
    assert f % nk == 0 and tk % LANES == 0
    grid_spec = pltpu.PrefetchScalarGridSpec(
        num_scalar_prefetch=2,
        grid=(nk, n // tn, rs // tm),
        in_specs=[pl.BlockSpec((tm, tk), lambda k, j, t, te, nv: (jnp.minimum(t, nv[0] - 1), k)),
                  pl.BlockSpec((None, None, tk, tn), lambda k, j, t, te, nv: (layer, te[t], k, j))],
        out_specs=pl.BlockSpec((None, tm, tn), lambda k, j, t, te, nv: (k, t, j)),
        scratch_shapes=[pltpu.VMEM((tk, tn), BF16)],
    )
    return pl.pallas_call(
        _moe_down_body,
        out_shape=jax.ShapeDtypeStruct((nk, rs, n), F32),
        grid_spec=grid_spec,
        compiler_params=_cparams("arbitrary", "arbitrary", "arbitrary"),
        name=name,
    )(tile_expert, n_valid, hs, w2)


def _slab_block_index(i, n_prompt_tiles, tiles_per_seq, n_prompt_blocks):
    return jnp.where(i < n_prompt_tiles, i // tiles_per_seq, n_prompt_blocks + i - n_prompt_tiles)


def _read_x(x_refs, n_prompt_tiles, tile0):
    if len(x_refs) == 1:
        return x_refs[0][...]
    return jnp.where(pl.program_id(0) + tile0 < n_prompt_tiles, x_refs[0][...], x_refs[1][...])


def _modulate_body(*refs, n_x, n_prompt_tiles):
    sc_ref, sh_ref, h_ref = refs[n_x:]
    x = _read_x(refs[:n_x], n_prompt_tiles, 0)
    h_ref[...] = (x * (1.0 + _slab_rows(sc_ref)) + _slab_rows(sh_ref)).astype(h_ref.dtype)


def _res_ln_body(*refs, alpha, eps, n_x, n_parts, emit_h, n_prompt_tiles, tile0):
    f_refs = refs[n_x:n_x + n_parts]
    gt_ref, sc_ref, sh_ref, g_ref, b_ref = refs[n_x + n_parts:n_x + n_parts + 5]
    outs = refs[n_x + n_parts + 5:]
    f = f_refs[0][...]
    for fr in f_refs[1:]:
        f = f + fr[...]
    y = alpha * _read_x(refs[:n_x], n_prompt_tiles, tile0) + _slab_rows(gt_ref) * f
    mu = jnp.mean(y, axis=-1, keepdims=True)
    yc = y - mu
    var = jnp.mean(yc * yc, axis=-1, keepdims=True)
    xn = yc * lax.rsqrt(var + eps) * g_ref[...] + b_ref[...]
    outs[0][...] = xn
    if emit_h:
        outs[1][...] = (xn * (1.0 + _slab_rows(sc_ref)) + _slab_rows(sh_ref)).astype(BF16)


class _Tiles:
    def __init__(self, bp, tp, bs, ts):
        self.rp, self.rs = bp * tp, bs * ts
        self.n_prompt = self.rp // CHUNK
        self.n_tiles = (self.rp + self.rs) // CHUNK
        self.per_seq = tp // CHUNK
        self.bp = bp

    def slab_spec(self, width, comp, tile0=0):
        idx = functools.partial(_slab_block_index, n_prompt_tiles=self.n_prompt, tiles_per_seq=self.per_seq,
                                n_prompt_blocks=self.bp)
        return pl.BlockSpec((SLABS, width), lambda i: (idx(i + tile0), comp))

    def row_spec(self, width, col_block=0, tile0=0):
        return pl.BlockSpec((CHUNK, width), lambda i: (i + tile0, col_block))

    def x_specs(self, x, width, tile0=0):
        if not isinstance(x, tuple):
            return [self.row_spec(width, tile0=tile0)], [x]
        n_p = self.n_prompt
        return [pl.BlockSpec((CHUNK, width), lambda i: (jnp.minimum(i + tile0, n_p - 1), 0)),
                pl.BlockSpec((CHUNK, width), lambda i: (jnp.maximum(i + tile0 - n_p, 0), 0))], list(x)


def _modulate(x, tab, comp_sc, comp_sh, tiles, d, *, name="modulate"):
    x_specs, x_args = tiles.x_specs(x, d)
    return pl.pallas_call(
        functools.partial(_modulate_body, n_x=len(x_args), n_prompt_tiles=tiles.n_prompt),
        out_shape=jax.ShapeDtypeStruct((tiles.n_tiles * CHUNK, d), BF16),
        grid=(tiles.n_tiles,),
        in_specs=x_specs + [tiles.slab_spec(d, comp_sc), tiles.slab_spec(d, comp_sh)],
        out_specs=tiles.row_spec(d),
        compiler_params=_cparams("parallel"),
        name=name,
    )(*x_args, tab, tab)


def _res_ln(x, f_parts, tab_gt, comp_gt, tab_mod, comp_sc, comp_sh, g, b, alpha, tiles, d, *, emit_h, name,
            tile_range=None):
    tile0, n_run = tile_range if tile_range is not None else (0, tiles.n_tiles)
    vec = pl.BlockSpec((1, d), lambda i: (0, 0))
    x_specs, x_args = tiles.x_specs(x, d, tile0)
    f_specs, f_args = [], []
    for fp in f_parts:
        if fp.ndim == 3:
            for k in range(fp.shape[0]):
                f_specs.append(pl.BlockSpec((None, CHUNK, d), lambda i, k=k: (k, i + tile0, 0)))
                f_args.append(fp)
        else:
            f_specs.append(tiles.row_spec(d, tile0=tile0))
            f_args.append(fp)
    out_shape = [jax.ShapeDtypeStruct((n_run * CHUNK, d), F32)]
    out_specs = [tiles.row_spec(d)]
    if emit_h:
        out_shape.append(jax.ShapeDtypeStruct((n_run * CHUNK, d), BF16))
        out_specs.append(tiles.row_spec(d))
    slab = lambda comp: tiles.slab_spec(d, comp, tile0)
    outs = pl.pallas_call(
        functools.partial(_res_ln_body, alpha=alpha, eps=LN_EPS, n_x=len(x_args), n_parts=len(f_args),
                          emit_h=emit_h, n_prompt_tiles=tiles.n_prompt, tile0=tile0),
        out_shape=out_shape,
        grid=(n_run,),
        in_specs=x_specs + f_specs + [slab(comp_gt), slab(comp_sc), slab(comp_sh), vec, vec],
        out_specs=out_specs,
        compiler_params=_cparams("parallel"),
        name=name,
    )(*x_args, *f_args, tab_gt, tab_mod, tab_mod, g.reshape(1, d), b.reshape(1, d))
    return outs if emit_h else (outs[0], None)


def _rwkv_prep_body(*refs, n_prompt_tiles, tiles_per_seq, first_layer):
    (r_ref, k_ref, v_ref, l_ref, pr_ref, pk_ref, pv_ref, pl_ref, sr_ref, sk_ref, sv_ref, sl_ref,
     mr_ref, mk_ref, mv_ref, ml_ref, w0_ref, a0_ref, kk_ref, ka_ref, w2_ref, a2_ref, g2_ref) = refs[:23]
    if first_layer:
        outs = refs[23:]
    else:
        v0_ref, v1_ref, v2_ref, vf_ref = refs[23:27]
        outs = refs[27:]
    ro_ref, wo_ref, ko_ref, vo_ref, ao_ref, bo_ref, go_ref = outs

    i = pl.program_id(0)
    is_sample = i >= n_prompt_tiles
    carry = jnp.logical_and(jnp.logical_not(is_sample), (i % tiles_per_seq) != 0)

    def shifted(x_ref, prev_ref, st_ref, mu_ref):
        x = x_ref[...]
        rowid = lax.broadcasted_iota(jnp.int32, x.shape, 0)
        prev = pltpu.roll(x, 1, 0)
        start_p = jnp.broadcast_to(jnp.where(carry, prev_ref[SUBLANES - 1:SUBLANES, :], 0.0), x.shape)
        start = jnp.where(is_sample, _slab_rows(st_ref), start_p)
        is_start = (rowid & jnp.where(is_sample, SUBLANES - 1, CHUNK - 1)) == 0
        prev = jnp.where(is_start, start, prev)
        return x + mu_ref[...] * (prev - x)

    ones_bd = _ones_block_diag()
    r = shifted(r_ref, pr_ref, sr_ref, mr_ref)
    k = shifted(k_ref, pk_ref, sk_ref, mk_ref)
    v = shifted(v_ref, pv_ref, sv_ref, mv_ref)
    xl = shifted(l_ref, pl_ref, sl_ref, ml_ref)

    w_lora = jnp.dot(jnp.tanh(xl).astype(BF16), w2_ref[...], preferred_element_type=F32)
    a_lora = jnp.dot(xl.astype(BF16), a2_ref[...], preferred_element_type=F32)
    go_ref[...] = jnp.dot(jax.nn.sigmoid(xl).astype(BF16), g2_ref[...], preferred_element_type=F32)

    z = -(w0_ref[...] + w_lora)
    softplus = jnp.maximum(z, 0.0) + jnp.log(1.0 + jnp.exp(-jnp.abs(z)))
    wo_ref[...] = jnp.exp(-jnp.exp(-softplus - 0.5))

    if not first_layer:
        mix = jnp.dot(jnp.dot(v.astype(BF16), v1_ref[...], preferred_element_type=F32).astype(BF16), v2_ref[...],
                      preferred_element_type=F32)
        v = v + (vf_ref[...] - v) * jax.nn.sigmoid(v0_ref[...] + mix)
    a = jax.nn.sigmoid(a0_ref[...] + a_lora)
    kk = k * kk_ref[...]
    kk = kk / jnp.maximum(jnp.sqrt(_head_sum(kk * kk, ones_bd)), 1e-12)
    ro_ref[...] = r
    ko_ref[...] = k * (1.0 + (a - 1.0) * ka_ref[...])
    vo_ref[...] = v
    ao_ref[...] = -kk
    bo_ref[...] = kk * a


def _rwkv_prep(proj, cols, shift_tab, mu_al, vecs, loras, vmix, tiles, d_a, *, name="rwkv_prep"):
    rws = proj.shape[0]
    first_layer = vmix is None
    c_r, c_k, c_v, c_l = cols
    assert c_r % d_a == 0 and c_k % d_a == 0 and c_v % d_a == 0 and c_l % LORA_PAD == 0
    assert (3 * d_a) % LORA_PAD == 0

    def prev_spec(width, cb):
        return pl.BlockSpec((SUBLANES, width), lambda i: (jnp.maximum(i * SLABS - 1, 0), cb))

    def start_spec(width, cb):
        return pl.BlockSpec((SLABS, width), lambda i: (jnp.maximum(i - tiles.n_prompt + 1, 0), cb))

    widths = (d_a, d_a, d_a, LORA_PAD)
    proj_cb = (c_r // d_a, c_k // d_a, c_v // d_a, c_l // LORA_PAD)
    tab_cb = (0, 1, 2, 3 * d_a // LORA_PAD)
    vec = lambda w: pl.BlockSpec((1, w), lambda i: (0, 0))
    full = lambda a: pl.BlockSpec(a.shape, lambda i: (0,) * a.ndim, pipeline_mode=pl.Buffered(1))
    in_specs = ([tiles.row_spec(w, cb) for w, cb in zip(widths, proj_cb)]
                + [prev_spec(w, cb) for w, cb in zip(widths, proj_cb)]
                + [start_spec(w, cb) for w, cb in zip(widths, tab_cb)]
                + [pl.BlockSpec((1, w), lambda i, cb=cb: (0, cb)) for w, cb in zip(widths, tab_cb)]
                + [vec(d_a)] * 4 + [full(w) for w in loras])
    args = [proj] * 8 + [shift_tab] * 4 + [mu_al] * 4 + list(vecs) + list(loras)
    if not first_layer:
        v0, v1p, v2p, v_first = vmix
        in_specs += [vec(d_a), full(v1p), full(v2p), tiles.row_spec(d_a)]
        args += [v0, v1p, v2p, v_first]
    out = jax.ShapeDtypeStruct((rws, d_a), F32)
    return pl.pallas_call(
        functools.partial(_rwkv_prep_body, n_prompt_tiles=tiles.n_prompt, tiles_per_seq=tiles.per_seq,
                          first_layer=first_layer),
        out_shape=[out] * 7,
        grid=(tiles.n_tiles,),
        in_specs=in_specs,
        out_specs=[tiles.row_spec(d_a)] * 7,
        compiler_params=_cparams("parallel"),
        name=name,
    )(*args)


def _last_slab_body(r_ref, k_ref, v_ref, l_ref, o_ref):
    col = 0
    for x_ref in (r_ref, k_ref, v_ref, l_ref):
        o_ref[:, col:col + x_ref.shape[-1]] = x_ref[...]
        col += x_ref.shape[-1]


def _last_slabs(proj, cols, bp, tp, bs, d_a, *, name="last_slab"):
    rp = bp * tp
    widths = (d_a, d_a, d_a, LORA_PAD)
    slab = lambda s: jnp.where(s < bp, (s + 1) * (tp // SUBLANES) - 1, rp // SUBLANES + s - bp)
    in_specs = [pl.BlockSpec((SUBLANES, w), lambda s, cb=c // w: (slab(s), cb)) for w, c in zip(widths, cols)]
    p_al = 3 * d_a + LORA_PAD
    return pl.pallas_call(
        _last_slab_body,
        out_shape=jax.ShapeDtypeStruct(((bp + bs) * SUBLANES, p_al), F32),
        grid=(bp + bs,),
        in_specs=in_specs,
        out_specs=pl.BlockSpec((SUBLANES, p_al), lambda s: (s, 0)),
        compiler_params=_cparams("parallel"),
        name=name,
    )(proj, proj, proj, proj)


def _rwkv_post_body(yp_ref, ys_ref, r_ref, k_ref, v_ref, g_ref, lg_ref, lb_ref, rk_ref, o_ref, *, n_prompt_tiles):
    ones_bd = _ones_block_diag()
    y = jnp.where(pl.program_id(0) < n_prompt_tiles, yp_ref[...], ys_ref[...])
    mu = _head_sum(y, ones_bd) * (1.0 / HEAD)
    yc = y - mu
    var = _head_sum(yc * yc, ones_bd) * (1.0 / HEAD)
    yn = yc * lax.rsqrt(var + GN_EPS) * lg_ref[...] + lb_ref[...]
    bonus = _head_sum(r_ref[...] * k_ref[...] * rk_ref[...], ones_bd) * v_ref[...]
    o_ref[...] = ((yn + bonus) * g_ref[...]).astype(o_ref.dtype)


def _rwkv_post(y_p, y_s, r, k, v, g, lnx_g, lnx_b, r_k, tiles, *, name="rwkv_post"):
    rws, d_a = r.shape
    vec = pl.BlockSpec((1, d_a), lambda i: (0, 0))
    n_p = tiles.n_prompt
    return pl.pallas_call(
        functools.partial(_rwkv_post_body, n_prompt_tiles=n_p),
        out_shape=jax.ShapeDtypeStruct((rws, d_a), BF16),
        grid=(tiles.n_tiles,),
        in_specs=[pl.BlockSpec((CHUNK, d_a), lambda i: (jnp.minimum(i, n_p - 1), 0)),
                  pl.BlockSpec((CHUNK, d_a), lambda i: (jnp.maximum(i - n_p, 0), 0))]
                 + [tiles.row_spec(d_a)] * 4 + [vec] * 3,
        out_specs=tiles.row_spec(d_a),
        compiler_params=_cparams("parallel"),
        name=name,
    )(y_p, y_s, r, k, v, g, lnx_g.reshape(1, d_a), lnx_b.reshape(1, d_a), r_k.reshape(1, d_a))


SEQS = 2


def _wkv_body(*refs, tb, n_pairs):
    in_refs = [refs[q * 6:(q + 1) * 6] for q in range(SEQS)]
    s0_ref, y_ref, sout_ref, s_scr = refs[SEQS * 6:]
    tblk = pl.program_id(1)
    rows = n_pairs * HEAD

    @pl.when(tblk == 0)
    def _():
        for q in range(SEQS):
            for p in range(n_pairs):
                s_scr[q, pl.ds(p * HEAD, HEAD), :] = jnp.concatenate(
                    [s0_ref[q, 2 * p], s0_ref[q, 2 * p + 1]], axis=-1)

    r2 = lax.broadcasted_iota(jnp.int32, (2 * LANES, 2 * LANES), 0)
    c2 = lax.broadcasted_iota(jnp.int32, (2 * LANES, 2 * LANES), 1)
    ones_bd2 = jnp.where((r2 // HEAD) == (c2 // HEAD), 1.0, 0.0).astype(BF16)
    rr = lax.broadcasted_iota(jnp.int32, (SUBLANES * LANES, LANES), 0)
    cc = lax.broadcasted_iota(jnp.int32, (SUBLANES * LANES, LANES), 1)
    step_sel = jnp.where(jnp.logical_and((rr % LANES) // HEAD == cc // HEAD, cc % HEAD == rr // LANES),
                         1.0, 0.0).astype(BF16)

    def bcast(row):
        return jnp.concatenate(
            [jnp.broadcast_to(row[:, p * LANES:(p + 1) * LANES], (HEAD, LANES)) for p in range(n_pairs)], axis=0)

    def head_sum(x):
        half = x.shape[0] // 2
        x2 = jnp.concatenate([x[:half], x[half:]], axis=1)
        o2 = jnp.dot(x2, ones_bd2, preferred_element_type=F32)
        return jnp.concatenate([o2[:, :LANES], o2[:, LANES:]], axis=0)

    sub = lax.broadcasted_iota(jnp.int32, (rows, LANES), 0)
    lane = lax.broadcasted_iota(jnp.int32, (rows, LANES), 1)
    diag_bf16 = jnp.where((sub % HEAD) == (lane % HEAD), 1.0, 0.0).astype(BF16)

    def step8(t8, carry):
        t0 = pl.multiple_of(t8 * SUBLANES, SUBLANES)
        tiles8 = [[ref[pl.ds(t0, SUBLANES), :] for ref in in_refs[q]] for q in range(SEQS)]
        sr = [[] for _ in range(SEQS)]
        for j in range(SUBLANES):
            for q in range(SEQS):
                r8, w8, k8, v8, a8, b8 = (x8[j:j + 1, :] for x8 in tiles8[q])
                s = s_scr[q]
                sa = head_sum((s * bcast(a8)).astype(BF16))
                v_col = head_sum(bcast(v8.astype(BF16)) * diag_bf16)
                s = s * bcast(w8) + sa * bcast(b8) + v_col * bcast(k8)
                s_scr[q] = s
                sr[q].append((s * bcast(r8)).astype(BF16))
        for q in range(SEQS):
            y_cols = jnp.dot(jnp.concatenate(sr[q], axis=1), step_sel, preferred_element_type=F32)
            for p in range(n_pairs):
                yt = y_cols[p * HEAD:(p + 1) * HEAD].T
                y_ref[q, pl.ds(t0, SUBLANES), pl.ds(p * LANES, LANES)] = jnp.concatenate(
                    [yt[:SUBLANES], yt[HEAD:HEAD + SUBLANES]], axis=1)
        return carry

    lax.fori_loop(0, tb // SUBLANES, step8, 0)

    @pl.when(tblk == pl.num_programs(1) - 1)
    def _():
        for q in range(SEQS):
            for p in range(n_pairs):
                sp = s_scr[q, pl.ds(p * HEAD, HEAD), :]
                sout_ref[q, 2 * p] = sp[:, :HEAD]
                sout_ref[q, 2 * p + 1] = sp[:, HEAD:]


def _wkv7(ins, s0, s0_layer, row0, bsz, t, *, name="wkv7"):
    da = ins[0].shape[-1]
    h = da // HEAD
    n_pairs = h // 2
    tb = _tile(t, 128, SUBLANES)
    nt = t // tb
    assert row0 % tb == 0 and bsz % SEQS == 0
    seq_in = [pl.BlockSpec((tb, da), lambda i, j, q=q: (row0 // tb + (SEQS * i + q) * nt + j, 0))
              for q in range(SEQS)]
    if s0_layer is None:
        st_in = pl.BlockSpec((SEQS, h, HEAD, HEAD), lambda i, j: (i, 0, 0, 0))
    else:
        st_in = pl.BlockSpec((None, SEQS, h, HEAD, HEAD), lambda i, j: (s0_layer, i, 0, 0, 0))
    st_out = pl.BlockSpec((SEQS, h, HEAD, HEAD), lambda i, j: (i, 0, 0, 0))
    return pl.pallas_call(
        functools.partial(_wkv_body, tb=tb, n_pairs=n_pairs),
        out_shape=(jax.ShapeDtypeStruct((bsz, t, da), F32), jax.ShapeDtypeStruct((bsz, h, HEAD, HEAD), F32)),
        grid=(bsz // SEQS, nt),
        in_specs=[seq_in[q] for q in range(SEQS) for _ in ins] + [st_in],
        out_specs=(pl.BlockSpec((SEQS, tb, da), lambda i, j: (i, j, 0)), st_out),
        scratch_shapes=[pltpu.VMEM((SEQS, n_pairs * HEAD, LANES), F32)],
        compiler_params=_cparams("parallel", "arbitrary"),
        name=name,
    )(*[a for q in range(SEQS) for a in ins], s0)


def _gelu(x):
    return 0.5 * x * (1.0 + lax.erf(x * 0.7071067811865476))


def _sgu_body(u_ref, v_ref, g_ref, b_ref, ws_ref, bias_ref, o_ref, vn_ref, *, n_groups):
    v = _gelu(v_ref[...])
    mu = jnp.mean(v, axis=-1, keepdims=True)
    vc = v - mu
    var = jnp.mean(vc * vc, axis=-1, keepdims=True)
    vn = vc * lax.rsqrt(var + LN_EPS) * g_ref[...] + b_ref[...]
    vn_ref[...] = vn
    for g in range(n_groups):
        cols = slice(g * GROUP_C, (g + 1) * GROUP_C)
        sv = jnp.dot(ws_ref[0, g], vn[:, cols].astype(BF16), preferred_element_type=F32)
        o_ref[:, cols] = (_gelu(u_ref[:, cols]) * (sv + bias_ref[0, :, cols])).astype(o_ref.dtype)


def _sgu(proj, c_u, c_v, ln_g, ln_b, ws2, bias2, tiles, d_b, *, name="sgu"):
    rws = proj.shape[0]
    g = d_b // GROUP_C
    n_p = tiles.n_prompt
    assert c_u % d_b == 0 and c_v % d_b == 0
    sel = lambda i: jnp.where(i < n_p, 0, 1)
    vec = pl.BlockSpec((1, d_b), lambda i: (0, 0))
    vn_spec = pl.BlockSpec((CHUNK, d_b), lambda i: (jnp.maximum(i - n_p, 0), 0))
    return pl.pallas_call(
        functools.partial(_sgu_body, n_groups=g),
        out_shape=(jax.ShapeDtypeStruct((rws, d_b), BF16), jax.ShapeDtypeStruct((tiles.rs, d_b), F32)),
        grid=(tiles.n_tiles,),
        in_specs=[tiles.row_spec(d_b, c_u // d_b), tiles.row_spec(d_b, c_v // d_b), vec, vec,
                  pl.BlockSpec((1, g, CHUNK, CHUNK), lambda i: (sel(i), 0, 0, 0)),
                  pl.BlockSpec((1, CHUNK, d_b), lambda i: (sel(i), 0, 0))],
        out_specs=(tiles.row_spec(d_b), vn_spec),
        compiler_params=_cparams("arbitrary"),
        name=name,
    )(proj, proj, ln_g.reshape(1, d_b), ln_b.reshape(1, d_b), ws2, bias2)


def _merge_body(xa_ref, xb_ref, wa_ref, wb_ref, ga_ref, gb_ref, o_ref):
    ya = jnp.dot(xa_ref[...], wa_ref[...].astype(BF16), preferred_element_type=F32)
    yb = jnp.dot(xb_ref[...], wb_ref[...].astype(BF16), preferred_element_type=F32)
    o_ref[...] = (jax.nn.sigmoid(ga_ref[...]) * ya + jax.nn.sigmoid(gb_ref[...]) * yb).astype(o_ref.dtype)


def _merge(xa, xb, w_oa, w_ob, layer, proj, c_ga, c_gb, *, tm=1024, tn=256, name="merge"):
    m, ka = xa.shape
    kb = xb.shape[-1]
    n = w_oa.shape[-1]
    tm = _tile(m, tm, 16)
    tn = _tile(n, tn, LANES)
    assert c_ga % tn == 0 and c_gb % tn == 0
    ba, bb = c_ga // tn, c_gb // tn
    return pl.pallas_call(
        _merge_body,
        out_shape=jax.ShapeDtypeStruct((m, n), BF16),
        grid=(m // tm, n // tn),
        in_specs=[pl.BlockSpec((tm, ka), lambda i, j: (i, 0)), pl.BlockSpec((tm, kb), lambda i, j: (i, 0)),
                  pl.BlockSpec((None, ka, tn), lambda i, j: (layer, 0, j)),
                  pl.BlockSpec((None, kb, tn), lambda i, j: (layer, 0, j)),
                  pl.BlockSpec((tm, tn), lambda i, j: (i, ba + j)), pl.BlockSpec((tm, tn), lambda i, j: (i, bb + j))],
        out_specs=pl.BlockSpec((tm, tn), lambda i, j: (i, j)),
        compiler_params=_cparams("parallel", "parallel"),
        name=name,
    )(xa, xb, w_oa, w_ob, proj, proj)


def _moe_route(logits, n_experts, tm):
    rws = logits.shape[0]
    top_logit, top_idx = lax.top_k(logits, TOP_K)
    top_w = jax.nn.softmax(top_logit, axis=-1)
    flat_e = top_idx.reshape(-1).astype(jnp.int32)
    n_slots = rws * TOP_K
    n_tiles = n_slots // tm + n_experts
    order = jnp.argsort(flat_e, stable=True).astype(jnp.int32)
    sorted_e = flat_e[order]
    counts = jnp.sum(jax.nn.one_hot(flat_e, n_experts, dtype=jnp.int32), axis=0)
    padded = ((counts + tm - 1) // tm) * tm
    ends_padded = jnp.cumsum(padded)
    starts_padded = ends_padded - padded
    starts = jnp.cumsum(counts) - counts
    rank = jnp.arange(n_slots, dtype=jnp.int32) - starts[sorted_e]
    dest_sorted = starts_padded[sorted_e] + rank
    dest = jnp.zeros((n_slots,), jnp.int32).at[order].set(dest_sorted)
    src_row = jnp.zeros((n_tiles * tm,), jnp.int32).at[dest_sorted].set(order // TOP_K)
    tile_start = jnp.arange(n_tiles, dtype=jnp.int32) * tm
    tile_expert = jnp.minimum(jnp.searchsorted(ends_padded, tile_start, side='right'),
                              n_experts - 1).astype(jnp.int32)
    n_valid = (ends_padded[-1] // tm).astype(jnp.int32).reshape(1)
    return top_w, dest.reshape(rws, TOP_K), src_row, tile_expert, n_valid


def kernel(x_prompt, x_sample, c_prompt, c_sample, state_wkv, state_shift, w_ada, b_ada, w_in, mu_shift, w0, w2, a0, a2, g2, v0, v1, v2, k_k, k_a, r_k, lnx_g, lnx_b, w_oa, sgu_ln_g, sgu_ln_b, w_s, b_s, w_ob, w_out, ln1_g, ln1_b, ffn_w1, ffn_w3, ffn_w2, w_router, moe_w1, moe_w3, moe_w2, ln2_g, ln2_b):
    bp, tp, d = x_prompt.shape
    bs, ts, _ = x_sample.shape
    depth = w_in.shape[0]
    h_a = state_wkv.shape[2]
    d_a = h_a * HEAD
    d_b = w_ob.shape[1]
    p_a = 3 * d_a + R_W + R_A + R_G
    p_al = 3 * d_a + LORA_PAD
    n_experts = moe_w1.shape[1]
    rp, rs = bp * tp, bs * ts
    alpha = (2 * depth) ** 0.25
    assert tp % CHUNK == 0 and ts == SUBLANES and rs % CHUNK == 0 and d_b % GROUP_C == 0
    tiles = _Tiles(bp, tp, bs, ts)

    c_all = jnp.concatenate([c_prompt, c_sample], axis=0)
    n_c = bp + bs
    c_pad = jnp.zeros((-(-n_c // 16) * 16, d), F32).at[:n_c].set(jax.nn.silu(c_all))
    mods = [_mm(c_pad, w_ada, layer=l, bias=b_ada[l], tn=1024, name="ada") for l in range(depth)]
    slab_tab = lambda a, n_lead: jnp.concatenate([jnp.repeat(a[:n_lead], SLABS, axis=0), a[n_lead:]], axis=0)
    mod_tabs = [slab_tab(m[:n_c], bp) for m in mods]
    SH1, SC1, GT1, SH2, SC2, GT2 = range(6)

    c_u, c_v, c_ga, c_gb = 0, d_b, 2 * d_b, 2 * d_b + d
    c_pa = 2 * d_b + 2 * d
    w_in_b = w_in.astype(BF16)
    w_in_al = jnp.concatenate(
        [w_in_b[..., p_a:], w_in_b[..., :p_a], jnp.zeros((depth, d, p_al - p_a), BF16)], axis=-1)
    pad_vec = lambda a: jnp.pad(a, [(0, 0)] * (a.ndim - 1) + [(0, p_al - p_a)])
    mu_al = pad_vec(mu_shift)

    causal = jnp.tril(jnp.ones((CHUNK, CHUNK), bool))
    causal_s = jnp.tril(jnp.ones((ts, ts), bool))
    eye_s = jnp.eye(CHUNK // ts, dtype=F32)
    pad_rows = lambda w, lo: jnp.zeros((LORA_PAD, w.shape[-1]), F32).at[lo:lo + w.shape[0]].set(w).astype(BF16)

    x = (x_prompt.reshape(rp, d), x_sample.reshape(rs, d))
    h = _modulate(x, mod_tabs[0], SC1, SH1, tiles, d)
    wkv_p, wkv_s, shift_p, shift_s, chunk_v = [], [], [], [], []
    v_first = None
    zeros_wkv = jnp.zeros((bp, h_a, HEAD, HEAD), F32)
    for l in range(depth):
        proj = _mm(h, w_in_al, layer=l, name="in_proj")
        shift_tab = jnp.concatenate([jnp.zeros((SLABS, p_al), F32), pad_vec(state_shift[l])], axis=0)
        vecs = [a[l].reshape(1, d_a) for a in (w0, a0, k_k, k_a)]
        loras = [pad_rows(w2[l], 0), pad_rows(a2[l], R_W), pad_rows(g2[l], R_W + R_A)]
        vmix = None
        if l > 0:
            v1p = jnp.zeros((d_a, LANES), F32).at[:, :R_V].set(v1[l - 1]).astype(BF16)
            v2p = jnp.zeros((LANES, d_a), F32).at[:R_V].set(v2[l - 1]).astype(BF16)
            vmix = (v0[l - 1].reshape(1, d_a), v1p, v2p, v_first)
        pa_cols = (c_pa, c_pa + d_a, c_pa + 2 * d_a, c_pa + 3 * d_a)
        r_s, decay, k_m, v_m, kk_neg, kk_a, g = _rwkv_prep(
            proj, pa_cols, shift_tab, mu_al[l].reshape(1, p_al), vecs, loras, vmix, tiles, d_a)
        seq_last = _last_slabs(proj, pa_cols, bp, tp, bs, d_a).reshape(bp + bs, SUBLANES, p_al)[:, -1, :p_a]
        shift_p.append(seq_last[:bp])
        shift_s.append(seq_last[bp:])
        if l == 0:
            v_first = v_m
        scan_in = (r_s, decay, k_m, v_m, kk_neg, kk_a)
        y_p, s_p = _wkv7(scan_in, zeros_wkv, None, 0, bp, tp, name="wkv_prompt")
        y_s, s_s = _wkv7(scan_in, state_wkv, l, rp, bs, ts, name="wkv_sample")
        wkv_p.append(s_p)
        wkv_s.append(s_s)
        xa = _rwkv_post(y_p.reshape(rp, d_a), y_s.reshape(rs, d_a), r_s, k_m, v_m, g, lnx_g[l], lnx_b[l], r_k[l], tiles)

        ws_p = jnp.where(causal[None], w_s[l], 0)
        ws_small = jnp.where(causal_s[None], w_s[l][:, :ts, :ts], 0)
        ws_s = jnp.einsum('ab,gts->gatbs', eye_s, ws_small).reshape(-1, CHUNK, CHUNK)
        ws2 = jnp.stack([ws_p, ws_s]).astype(BF16)
        bias_p = jnp.repeat(jnp.swapaxes(b_s[l], 0, 1), GROUP_C, axis=-1)
        bias_s = jnp.repeat(jnp.tile(jnp.swapaxes(b_s[l][:, :ts], 0, 1), (CHUNK // ts, 1)), GROUP_C, axis=-1)
        xb, vn_s = _sgu(proj, c_u, c_v, sgu_ln_g[l], sgu_ln_b[l], ws2, jnp.stack([bias_p, bias_s]), tiles, d_b)
        chunk_v.append(vn_s.reshape(bs, ts, d_b))

        merged = _merge(xa, xb, w_oa, w_ob, l, proj, c_ga, c_gb)
        mix = _mm(merged, w_out, layer=l, name="out_mix")
        x, h2 = _res_ln(x, [mix], mod_tabs[l], GT1, mod_tabs[l], SC2, SH2, ln1_g[l], ln1_b[l], alpha, tiles, d,
                        emit_h=True, name="ln1")

        i = l // 2
        if l % 2 == 0:
            f_parts = [_mm_ksplit(_swiglu(h2, ffn_w1, ffn_w3, i), ffn_w2, i, name="ffn_down")]
        else:
            tm_up, tm_down = 1024, 512
            w_r = jnp.zeros((d, LANES), F32).at[:, :n_experts].set(w_router[i])
            logits = _mm(h2, w_r, name="router")[:, :n_experts]
            top_w, dest, src_row, tile_expert, n_valid = _moe_route(logits, n_experts, tm_up)
            xs = jnp.take(h2, src_row, axis=0)
            hs = _moe_up(xs, moe_w1, moe_w3, i, tile_expert, n_valid, tm=tm_up)
            ys = _moe_down(hs, moe_w2, i, jnp.repeat(tile_expert, tm_up // tm_down), n_valid * (tm_up // tm_down),
                           tm=tm_down)
            ys = ys[0] + ys[1]
            f_parts = [top_w[:, 0:1] * jnp.take(ys, dest[:, 0], axis=0)
                       + top_w[:, 1:2] * jnp.take(ys, dest[:, 1], axis=0)]
        ln2 = functools.partial(_res_ln, x, f_parts, mod_tabs[l], GT2, mod_tabs[min(l + 1, depth - 1)], SC1, SH1,
                                ln2_g[l], ln2_b[l], alpha, tiles, d)
        if l + 1 < depth:
            x, h = ln2(emit_h=True, name="ln2")
        else:
            y_prompt, _ = ln2(emit_h=False, name="ln2_prompt", tile_range=(0, tiles.n_prompt))
            y_sample, _ = ln2(emit_h=False, name="ln2_sample",
                              tile_range=(tiles.n_prompt, tiles.n_tiles - tiles.n_prompt))

    return (y_prompt.reshape(bp, tp, d), y_sample.reshape(bs, ts, d), jnp.stack(wkv_p), jnp.stack(shift_p),
            jnp.stack(wkv_s), jnp.stack(shift_s), jnp.stack(chunk_v))
```
